```python
import jax
import jax.numpy as jnp
from jax import lax
import numpy as np

D_MODEL = 4096
BATCH = 1
SEQ = 8192
DEPTH = 2

GRID_W = 64
CTX_LEN = 256
HEAD_DIM = 128
ATTN_W = D_MODEL // 2
N_Q_HEADS = ATTN_W // HEAD_DIM
N_KV_HEADS = N_Q_HEADS // 4
Q_PER_KV = N_Q_HEADS // N_KV_HEADS
KV_W = N_KV_HEADS * HEAD_DIM
ROPE_THETA = 10000.0
Q_BLOCK = 128
CONV_W = D_MODEL // 4
RWKV_W = D_MODEL // 4
RWKV_HEAD = 64
RWKV_HEADS = RWKV_W // RWKV_HEAD
DECAY_LORA = 64
ICLR_LORA = 64
GATE_LORA = 160
GN_EPS = 64e-5
D_FF = 256 * ((8 * D_MODEL // 3 + 255) // 256)
N_BRANCH = 3
N_MOD = 6
NORM_EPS = 1e-6
PROJ_SIZES = (ATTN_W, KV_W, KV_W, CONV_W, CONV_W, CONV_W, 3 * RWKV_W, 2 * DECAY_LORA, 2 * ICLR_LORA, GATE_LORA, N_BRANCH * D_MODEL)

kernel_name = 'hybrid_gqa_shortconv_rwkv7_ctxprefix_dit'


def rms_norm(x, g):
    xf = x.astype(jnp.float32)
    y = xf * lax.rsqrt(jnp.mean(xf * xf, axis=-1, keepdims=True) + NORM_EPS)
    return (y * g.astype(jnp.float32)).astype(x.dtype)


def ada_mod(cond, w_mod, b_mod):
    m = jax.nn.silu(cond) @ w_mod + b_mod
    return jnp.split(m[..., None, :], N_MOD, axis=-1)


def modulate(h, shift, scale):
    return h * (1 + scale) + shift


def dwconv3(x, w):
    xp = jnp.pad(x, ((0, 0), (1, 1), (0, 0)))
    return xp[:, :-2] * w[0] + xp[:, 1:-1] * w[1] + xp[:, 2:] * w[2]


def split_proj(z):
    idx, acc = [], 0
    for s in PROJ_SIZES[:-1]:
        acc += s
        idx.append(acc)
    return jnp.split(z, idx, axis=-1)


def to_heads(z, head_dim):
    return z.reshape(z.shape[0], z.shape[1], -1, head_dim)


def axial_rope_tables(n_tokens):
    rows = n_tokens // GRID_W
    row = jnp.repeat(jnp.arange(rows, dtype=jnp.float32), GRID_W)
    col = jnp.tile(jnp.arange(GRID_W, dtype=jnp.float32), rows)
    half = HEAD_DIM // 2
    inv_freq = ROPE_THETA ** (-jnp.arange(0, half, 2, dtype=jnp.float32) / half)
    ang = jnp.concatenate([row[:, None] * inv_freq, col[:, None] * inv_freq], axis=-1)
    return jnp.cos(ang), jnp.sin(ang)


def _rotate(x, cos, sin):
    x1, x2 = jnp.split(x, 2, axis=-1)
    return jnp.concatenate([x1 * cos - x2 * sin, x1 * sin + x2 * cos], axis=-1)


def apply_axial_rope(x, cos, sin):
    half, m = HEAD_DIM // 2, HEAD_DIM // 4
    cos = cos[None, :, None, :].astype(x.dtype)
    sin = sin[None, :, None, :].astype(x.dtype)
    x_row = _rotate(x[..., :half], cos[..., :m], sin[..., :m])
    x_col = _rotate(x[..., half:], cos[..., m:], sin[..., m:])
    return jnp.concatenate([x_row, x_col], axis=-1)


def gqa_attention(q, k, v):
    b, tq = q.shape[0], q.shape[1]
    qg = q.reshape(b, tq, N_KV_HEADS, Q_PER_KV, HEAD_DIM)
    s = jnp.einsum('bqhgd,bkhd->bhgqk', qg, k).astype(jnp.float32) * (HEAD_DIM ** -0.5)
    p = jax.nn.softmax(s, axis=-1).astype(v.dtype)
    o = jnp.einsum('bhgqk,bkhd->bqhgd', p, v)
    return o.reshape(b, tq, ATTN_W)


def blocked_attention(q, k, v):
    b, tq = q.shape[0], q.shape[1]
    nb = tq // Q_BLOCK
    qb = jnp.moveaxis(q.reshape(b, nb, Q_BLOCK, N_Q_HEADS, HEAD_DIM), 1, 0)
    ob = lax.map(lambda q_blk: gqa_attention(q_blk, k, v), qb)
    return jnp.moveaxis(ob, 0, 1).reshape(b, tq, ATTN_W)


def short_conv_branch(h_in, gate_b, gate_c, conv_w, w_o):
    return (gate_b * dwconv3(gate_c * h_in, conv_w)) @ w_o


def rwkv_features(rkv, lw, la, lg, p):
    b, t = rkv.shape[0], rkv.shape[1]
    rkv = dwconv3(rkv, p['rwkv_conv_w'])
    r, k, v = jnp.split(rkv, 3, axis=-1)
    lw = lw.reshape(b, t, 2, DECAY_LORA)
    la = la.reshape(b, t, 2, ICLR_LORA)
    w = -jax.nn.softplus(-(p['rwkv_w0'] + jnp.einsum('btdr,drc->btdc', jnp.tanh(lw), p['rwkv_w2']))) - 0.5
    decay = jnp.exp(-jnp.exp(w.astype(jnp.float32)))
    a = jax.nn.sigmoid(p['rwkv_a0'] + jnp.einsum('btdr,drc->btdc', la, p['rwkv_a2']))
    kk = (k * p['rwkv_k_k']).reshape(b, t, RWKV_HEADS, RWKV_HEAD).astype(jnp.float32)
    kk = kk / jnp.maximum(jnp.sqrt(jnp.sum(kk * kk, axis=-1, keepdims=True)), 1e-12)
    k_dir = k[:, :, None] * (1 + (a - 1) * p['rwkv_k_a'])
    g = jax.nn.sigmoid(lg) @ p['rwkv_g2']
    hd = lambda z: z.reshape(*z.shape[:-1], RWKV_HEADS, RWKV_HEAD)
    return (hd(r), hd(v), kk, hd(decay), hd(k_dir), hd(a), g)


def wkv_scan(r, w, k, v, kk, a, s0, reverse):
    xs = tuple(jnp.moveaxis(z.astype(jnp.float32), 1, 0) for z in (r, w, k, v, kk, a))

    def step(S, inp):
        r_t, w_t, k_t, v_t, kk_t, a_t = inp
        sa = jnp.einsum('bhvk,bhk->bhv', S, kk_t)
        S = S * w_t[:, :, None, :] - sa[..., None] * (kk_t * a_t)[:, :, None, :] + v_t[..., None] * k_t[:, :, None, :]
        return S, jnp.einsum('bhvk,bhk->bhv', S, r_t)

    s_final, ys = lax.scan(step, s0, xs, reverse=reverse)
    return jnp.moveaxis(ys, 0, 1), s_final


def wkv_bidir(feat, s_fwd, s_bwd):
    r, v, kk, decay, k_dir, a, _ = feat
    y_f, s_f = wkv_scan(r, decay[:, :, 0], k_dir[:, :, 0], v, kk, a[:, :, 0], s_fwd, False)
    y_b, s_b = wkv_scan(r, decay[:, :, 1], k_dir[:, :, 1], v, kk, a[:, :, 1], s_bwd, True)
    return y_f + y_b, s_f, s_b


def rwkv_readout(y, feat, p):
    r, v, kk, decay, k_dir, a, g = feat
    b, t = v.shape[0], v.shape[1]
    mu = jnp.mean(y, axis=-1, keepdims=True)
    var = jnp.mean(jnp.square(y - mu), axis=-1, keepdims=True)
    yn = ((y - mu) * lax.rsqrt(var + GN_EPS)).reshape(b, t, RWKV_W)
    yn = (yn * p['rwkv_ln_w'].astype(jnp.float32) + p['rwkv_ln_b'].astype(jnp.float32)).astype(v.dtype)
    bonus = jnp.sum(r[:, :, None] * k_dir * p['rwkv_r_k'], axis=-1, keepdims=True) * v[:, :, None]
    out = yn + jnp.sum(bonus, axis=2).reshape(b, t, RWKV_W)
    return (out * g) @ p['w_rwkv_o']


def merge_branches(y_attn, y_conv, y_rwkv, gate_logits, b_gate, w_out):
    g_attn, g_conv, g_rwkv = jnp.split(jax.nn.sigmoid(gate_logits + b_gate), N_BRANCH, axis=-1)
    return (g_attn * y_attn + g_conv * y_conv + g_rwkv * y_rwkv) @ w_out


def conv_ffn(h, w_up, conv_w, conv_b, w_down):
    u = dwconv3(h @ w_up, conv_w) + conv_b
    val, gate = jnp.split(u, 2, axis=-1)
    return (jax.nn.silu(gate) * val) @ w_down


def hybrid_layer(xl, xc, c, c_ctx, cos, sin, p, last):
    mod_l = ada_mod(c, p['w_mod'], p['b_mod'])
    mod_c = ada_mod(c_ctx, p['w_mod'], p['b_mod'])
    hl = modulate(rms_norm(xl, p['norm1_g']), mod_l[0], mod_l[1])
    hc = modulate(rms_norm(xc, p['norm1_g']), mod_c[0], mod_c[1])
    pl = split_proj(hl @ p['w_in'])
    pc = split_proj(hc @ p['w_in'])

    kc = rms_norm(to_heads(pc[1], HEAD_DIM), p['k_norm_g'])
    vc = to_heads(pc[2], HEAD_DIM)
    kl = apply_axial_rope(rms_norm(to_heads(pl[1], HEAD_DIM), p['k_norm_g']), cos, sin)
    vl = to_heads(pl[2], HEAD_DIM)
    ql = apply_axial_rope(rms_norm(to_heads(pl[0], HEAD_DIM), p['q_norm_g']), cos, sin)
    k_all = jnp.concatenate([kc, kl], axis=1)
    v_all = jnp.concatenate([vc, vl], axis=1)
    ya_l = blocked_attention(ql, k_all, v_all) @ p['w_attn_o']

    yc_l = short_conv_branch(pl[3], pl[4], pl[5], p['conv_w'], p['w_conv_o'])

    feat_c = rwkv_features(pc[6], pc[7], pc[8], pc[9], p)
    feat_l = rwkv_features(pl[6], pl[7], pl[8], pl[9], p)
    s0 = jnp.zeros((xc.shape[0], RWKV_HEADS, RWKV_HEAD, RWKV_HEAD), jnp.float32)
    y_ctx, s_f, s_b = wkv_bidir(feat_c, s0, s0)
    y_lat, _, _ = wkv_bidir(feat_l, s_f, s_b)
    yr_l = rwkv_readout(y_lat, feat_l, p)

    xl = xl + mod_l[2] * merge_branches(ya_l, yc_l, yr_l, pl[10], p['b_gate'], p['w_out'])
    hl2 = modulate(rms_norm(xl, p['norm2_g']), mod_l[3], mod_l[4])
    xl = xl + mod_l[5] * conv_ffn(hl2, p['ffn_w_up'], p['ffn_conv_w'], p['ffn_conv_b'], p['ffn_w_down'])
    if last:
        return xl, xc

    qc = rms_norm(to_heads(pc[0], HEAD_DIM), p['q_norm_g'])
    ya_c = gqa_attention(qc, kc, vc) @ p['w_attn_o']
    yc_c = short_conv_branch(pc[3], pc[4], pc[5], p['conv_w'], p['w_conv_o'])
    yr_c = rwkv_readout(y_ctx, feat_c, p)
    xc = xc + mod_c[2] * merge_branches(ya_c, yc_c, yr_c, pc[10], p['b_gate'], p['w_out'])
    hc2 = modulate(rms_norm(xc, p['norm2_g']), mod_c[3], mod_c[4])
    xc = xc + mod_c[5] * conv_ffn(hc2, p['ffn_w_up'], p['ffn_conv_w'], p['ffn_conv_b'], p['ffn_w_down'])
    return xl, xc


def setup_inputs(seed: int = 0) -> dict:
    key = jax.random.key(seed)
    ks = iter(jax.random.split(key, 48))
    f32 = jnp.float32
    L = DEPTH
    proj_w = sum(PROJ_SIZES)

    def nrm(shape, scale):
        return scale * jax.random.normal(next(ks), shape, f32)

    def gain(shape):
        return 1.0 + 0.02 * jax.random.normal(next(ks), shape, f32)

    return {
        'x': nrm((BATCH, SEQ, D_MODEL), 1.0),
        'c': nrm((BATCH, D_MODEL), 1.0),
        'ctx': nrm((BATCH, CTX_LEN, D_MODEL), 1.0),
        'c_ctx': nrm((D_MODEL,), 1.0),
        'w_mod': nrm((L, D_MODEL, N_MOD * D_MODEL), 0.5 * D_MODEL ** -0.5),
        'b_mod': nrm((L, N_MOD * D_MODEL), 0.01),
        'norm1_g': gain((L, D_MODEL)),
        'norm2_g': gain((L, D_MODEL)),
        'w_in': nrm((L, D_MODEL, proj_w), D_MODEL ** -0.5),
        'b_gate': nrm((L, N_BRANCH * D_MODEL), 0.01),
        'q_norm_g': gain((L, HEAD_DIM)),
        'k_norm_g': gain((L, HEAD_DIM)),
        'w_attn_o': nrm((L, ATTN_W, D_MODEL), ATTN_W ** -0.5),
        'conv_w': nrm((L, 3, CONV_W), 3 ** -0.5),
        'w_conv_o': nrm((L, CONV_W, D_MODEL), CONV_W ** -0.5),
        'rwkv_conv_w': nrm((L, 3, 3 * RWKV_W), 3 ** -0.5),
        'rwkv_w0': jax.random.uniform(next(ks), (L, 2, RWKV_W), f32, -6.0, -1.0),
        'rwkv_w2': nrm((L, 2, DECAY_LORA, RWKV_W), 0.1 * DECAY_LORA ** -0.5),
        'rwkv_a0': nrm((L, 2, RWKV_W), 0.1),
        'rwkv_a2': nrm((L, 2, ICLR_LORA, RWKV_W), 0.1 * ICLR_LORA ** -0.5),
        'rwkv_g2': nrm((L, GATE_LORA, RWKV_W), GATE_LORA ** -0.5),
        'rwkv_k_k': 0.85 + 0.02 * jax.random.normal(next(ks), (L, RWKV_W), f32),
        'rwkv_k_a': gain((L, RWKV_W)),
        'rwkv_r_k': nrm((L, RWKV_HEADS, RWKV_HEAD), 0.1),
        'rwkv_ln_w': gain((L, RWKV_W)),
        'rwkv_ln_b': nrm((L, RWKV_W), 0.01),
        'w_rwkv_o': nrm((L, RWKV_W, D_MODEL), RWKV_W ** -0.5),
        'w_out': nrm((L, D_MODEL, D_MODEL), D_MODEL ** -0.5),
        'ffn_w_up': nrm((L, D_MODEL, 2 * D_FF), D_MODEL ** -0.5),
        'ffn_conv_w': nrm((L, 3, 2 * D_FF), 3 ** -0.5),
        'ffn_conv_b': nrm((L, 2 * D_FF), 0.01),
        'ffn_w_down': nrm((L, D_FF, D_MODEL), D_FF ** -0.5),
        'final_g': gain((D_MODEL,)),
    }


def reference(x, c, ctx, c_ctx, w_mod, b_mod, norm1_g, norm2_g, w_in, b_gate, q_norm_g, k_norm_g,
              w_attn_o, conv_w, w_conv_o, rwkv_conv_w, rwkv_w0, rwkv_w2, rwkv_a0, rwkv_a2, rwkv_g2,
              rwkv_k_k, rwkv_k_a, rwkv_r_k, rwkv_ln_w, rwkv_ln_b, w_rwkv_o, w_out,
              ffn_w_up, ffn_conv_w, ffn_conv_b, ffn_w_down, final_g):
    cos, sin = axial_rope_tables(x.shape[1])
    xl, xc = x, ctx
    for i in range(DEPTH):
        p = {
            'w_mod': w_mod[i], 'b_mod': b_mod[i], 'norm1_g': norm1_g[i], 'norm2_g': norm2_g[i],
            'w_in': w_in[i], 'b_gate': b_gate[i], 'q_norm_g': q_norm_g[i], 'k_norm_g': k_norm_g[i],
            'w_attn_o': w_attn_o[i], 'conv_w': conv_w[i], 'w_conv_o': w_conv_o[i],
            'rwkv_conv_w': rwkv_conv_w[i], 'rwkv_w0': rwkv_w0[i], 'rwkv_w2': rwkv_w2[i],
            'rwkv_a0': rwkv_a0[i], 'rwkv_a2': rwkv_a2[i], 'rwkv_g2': rwkv_g2[i],
            'rwkv_k_k': rwkv_k_k[i], 'rwkv_k_a': rwkv_k_a[i], 'rwkv_r_k': rwkv_r_k[i],
            'rwkv_ln_w': rwkv_ln_w[i], 'rwkv_ln_b': rwkv_ln_b[i], 'w_rwkv_o': w_rwkv_o[i],
            'w_out': w_out[i], 'ffn_w_up': ffn_w_up[i], 'ffn_conv_w': ffn_conv_w[i],
            'ffn_conv_b': ffn_conv_b[i], 'ffn_w_down': ffn_w_down[i],
        }
        xl, xc = hybrid_layer(xl, xc, c, c_ctx, cos, sin, p, i == DEPTH - 1)
    return rms_norm(xl, final_g)
```

```python
import functools

import jax
import jax.numpy as jnp
from jax import lax
from jax.experimental import pallas as pl
from jax.experimental.pallas import tpu as pltpu

F32 = jnp.float32
BF16 = jnp.bfloat16

GRID_W = 64
ROPE_THETA = 10000.0
Q_PER_KV = 4
N_MOD = 6
NORM_EPS = 1e-6
GN_EPS = 64e-5

LANES = 128
SUBLANES_F32 = 8
SUBLANES_BF16 = 16
VMEM_LIMIT_CAP = 56 * 1024 * 1024
CHUNK = 64
ROW_TILE = 256


def _pick(n, candidates):
    for c in candidates:
        if n % c == 0:
            return c
    raise ValueError(f"no tile in {candidates} divides {n}")


def _cparams(sem, vmem_bytes):
    limit = int(min(max(vmem_bytes * 5 // 4 + (4 << 20), 16 << 20), VMEM_LIMIT_CAP))
    return pltpu.CompilerParams(dimension_semantics=sem, vmem_limit_bytes=limit)


def _sigmoid(x):
    return 1.0 / (1.0 + jnp.exp(-x))


def _silu(x):
    return x * _sigmoid(x)


def _softplus(x):
    return jnp.maximum(x, 0.0) + jnp.log(1.0 + jnp.exp(-jnp.abs(x)))


def _mod_kernel(c_ref, w_ref, b_ref, o_ref):
    a = _silu(c_ref[...]).astype(BF16)
    o_ref[...] = jnp.dot(a, w_ref[...].astype(BF16), preferred_element_type=F32) + b_ref[...]


def _ada_mod(cond, w_mod, b_mod):
    L, D, N = w_mod.shape
    tn = _pick(N, (512, 256, 128))
    vm = 2 * (D * tn * 4 + 16 * tn * 4 * 2) + 16 * D * 4 * 2 + D * tn * 2
    return pl.pallas_call(
        _mod_kernel,
        grid=(L, N // tn),
        in_specs=[
            pl.BlockSpec((16, D), lambda l, j: (0, 0)),
            pl.BlockSpec((None, D, tn), lambda l, j: (l, 0, j)),
            pl.BlockSpec((None, 1, tn), lambda l, j: (l, 0, j)),
        ],
        out_specs=pl.BlockSpec((None, 16, tn), lambda l, j: (l, 0, j)),
        out_shape=jax.ShapeDtypeStruct((L, 16, N), F32),
        compiler_params=_cparams(("parallel", "parallel"), vm),
        name="ada_mod",
    )(cond, w_mod, b_mod.reshape(L, 1, N))


def _norm_mod_kernel(x_ref, g_ref, sh_ref, sc_ref, o_ref):
    x = x_ref[...]
    y = x * lax.rsqrt(jnp.mean(x * x, axis=-1, keepdims=True) + NORM_EPS) * g_ref[...]
    o_ref[...] = (y * (1.0 + sc_ref[...]) + sh_ref[...]).astype(o_ref.dtype)


def _norm_mod(x, g, shift2, scale2, n_lat, rows):
    D = x.shape[1]
    tr = ROW_TILE
    nl = n_lat // tr
    vm = 2 * (tr * D * 4 + tr * D * 2) + 6 * D * 4
    return pl.pallas_call(
        _norm_mod_kernel,
        grid=(rows // tr,),
        in_specs=[
            pl.BlockSpec((tr, D), lambda i: (i, 0)),
            pl.BlockSpec((1, D), lambda i: (0, 0)),
            pl.BlockSpec((None, 1, D), lambda i: (jnp.where(i >= nl, 1, 0), 0, 0)),
            pl.BlockSpec((None, 1, D), lambda i: (jnp.where(i >= nl, 1, 0), 0, 0)),
        ],
        out_specs=pl.BlockSpec((tr, D), lambda i: (i, 0)),
        out_shape=jax.ShapeDtypeStruct((rows, D), BF16),
        compiler_params=_cparams(("parallel",), vm),
        name="norm_mod",
    )(x, g.reshape(1, D), shift2, scale2)


def _final_norm_kernel(x_ref, g_ref, o_ref):
    x = x_ref[...]
    o_ref[...] = x * lax.rsqrt(jnp.mean(x * x, axis=-1, keepdims=True) + NORM_EPS) * g_ref[...]


def _final_norm(x, g, rows):
    D = x.shape[1]
    tr = ROW_TILE
    return pl.pallas_call(
        _final_norm_kernel,
        grid=(rows // tr,),
        in_specs=[pl.BlockSpec((tr, D), lambda i: (i, 0)), pl.BlockSpec((1, D), lambda i: (0, 0))],
        out_specs=pl.BlockSpec((tr, D), lambda i: (i, 0)),
        out_shape=jax.ShapeDtypeStruct((rows, D), F32),
        compiler_params=_cparams(("parallel",), 4 * tr * D * 4),
        name="final_norm",
    )(x, g.reshape(1, D))


def _mm_kernel(*refs, nk, epi, tm, n_lat):
    if nk > 1:
        acc_ref = refs[-1]
        refs = refs[:-1]
    a_ref, b_ref = refs[0], refs[1]
    o_ref = refs[-1]
    extra = refs[2:-1]
    row0 = pl.program_id(0) * tm

    def finish(acc):
        if epi == "cast":
            o_ref[...] = acc.astype(o_ref.dtype)
        elif epi == "sigmoid_bias":
            o_ref[...] = _sigmoid(acc + extra[0][...]).astype(o_ref.dtype)
        elif epi == "residual":
            res_ref, gate_ref = extra
            row = row0 + lax.broadcasted_iota(jnp.int32, acc.shape, 0)
            gate = jnp.where(row < n_lat, gate_ref[0:1, :], gate_ref[1:2, :])
            o_ref[...] = res_ref[...] + gate * acc
        else:
            raise ValueError(epi)

    part = jnp.dot(a_ref[...], b_ref[...], preferred_element_type=F32)
    if nk == 1:
        finish(part)
        return
    k = pl.program_id(2)

    @pl.when(k == 0)
    def _():
        acc_ref[...] = part

    @pl.when(k > 0)
    def _():
        acc_ref[...] += part

    @pl.when(k == nk - 1)
    def _():
        finish(acc_ref[...])


def _matmul(a, b, *, rows, out_dtype, epi="cast", bias=None, res=None, gate2=None, n_lat=0, name="mm"):
    K, N = b.shape
    tm = _pick(rows, (768, 512, 256, 128, 64))
    tn = _pick(N, (1024, 512, 256, 128))
    tk = K if K <= 4096 else _pick(K, (1024, 512, 256, 128))
    nk = K // tk
    osz = jnp.dtype(out_dtype).itemsize
    in_specs = [
        pl.BlockSpec((tm, tk), lambda i, j, k: (i, k)),
        pl.BlockSpec((tk, tn), lambda i, j, k: (k, j)),
    ]
    args = [a, b]
    vm = 2 * (tm * tk * 2 + tk * tn * 2 + tm * tn * osz) + 2 * tm * tn * 4
    if epi == "sigmoid_bias":
        in_specs.append(pl.BlockSpec((1, tn), lambda i, j, k: (0, j)))
        args.append(bias.reshape(1, N))
    elif epi == "residual":
        in_specs.append(pl.BlockSpec((tm, tn), lambda i, j, k: (i, j)))
        in_specs.append(pl.BlockSpec((2, tn), lambda i, j, k: (0, j)))
        args += [res, gate2]
        vm += 2 * tm * tn * 4
    scratch = [pltpu.VMEM((tm, tn), F32)] if nk > 1 else []
    return pl.pallas_call(
        functools.partial(_mm_kernel, nk=nk, epi=epi, tm=tm, n_lat=n_lat),
        grid=(rows // tm, N // tn, nk),
        in_specs=in_specs,
        out_specs=pl.BlockSpec((tm, tn), lambda i, j, k: (i, j)),
        out_shape=jax.ShapeDtypeStruct((rows, N), out_dtype),
        scratch_shapes=scratch,
        compiler_params=_cparams(("parallel", "parallel", "arbitrary"), vm),
        name=name,
    )(*args)


def _qk_prep_kernel(zq_ref, zk_ref, c_ref, sa_ref, sb_ref, gq_ref, gk_ref, q_ref, k_ref, *, hd):
    cos, sa, sb = c_ref[...], sa_ref[...], sb_ref[...]

    def norm_rope(x, g):
        y = x * lax.rsqrt(jnp.mean(x * x, axis=-1, keepdims=True) + NORM_EPS) * g
        return y * cos + pltpu.roll(y, hd - hd // 4, 1) * sa + pltpu.roll(y, hd // 4, 1) * sb

    gq = gq_ref[...] * (hd ** -0.5)
    for h in range(zq_ref.shape[1] // hd):
        sl = slice(h * hd, (h + 1) * hd)
        q_ref[:, sl] = norm_rope(zq_ref[:, sl].astype(F32), gq).astype(BF16)
    gk = gk_ref[...]
    for h in range(zk_ref.shape[1] // hd):
        sl = slice(h * hd, (h + 1) * hd)
        k_ref[:, sl] = norm_rope(zk_ref[:, sl].astype(F32), gk).astype(BF16)


def _qk_prep(z1, tabs, gq, gk, aw, kvw, hd):
    R = z1.shape[0]
    tr = ROW_TILE
    assert aw % kvw == 0
    cos, sa, sb = tabs
    tab_spec = pl.BlockSpec((tr, hd), lambda i: (i, 0))
    vm = 2 * 2 * (tr * aw * 2 + tr * kvw * 2) + 6 * tr * hd * 4 + 8 * tr * hd * 4
    return pl.pallas_call(
        functools.partial(_qk_prep_kernel, hd=hd),
        grid=(R // tr,),
        in_specs=[
            pl.BlockSpec((tr, aw), lambda i: (i, 0)),
            pl.BlockSpec((tr, kvw), lambda i: (i, aw // kvw)),
            tab_spec, tab_spec, tab_spec,
            pl.BlockSpec((1, hd), lambda i: (0, 0)),
            pl.BlockSpec((1, hd), lambda i: (0, 0)),
        ],
        out_specs=[pl.BlockSpec((tr, aw), lambda i: (i, 0)), pl.BlockSpec((tr, kvw), lambda i: (i, 0))],
        out_shape=[jax.ShapeDtypeStruct((R, aw), BF16), jax.ShapeDtypeStruct((R, kvw), BF16)],
        compiler_params=_cparams(("parallel",), vm),
        name="qk_prep",
    )(z1, z1, cos, sa, sb, gq.reshape(1, hd), gk.reshape(1, hd))


def _attn_kernel(q_ref, k_ref, v_ref, o_ref, *, hd, tk, nkv):
    tq = q_ref.shape[0]
    q4 = jnp.concatenate([q_ref[:, g * hd:(g + 1) * hd] for g in range(Q_PER_KV)], axis=0)

    def step(j, carry):
        m, l, acc = carry
        off = pl.multiple_of(j * tk, tk)
        kc = k_ref[pl.ds(off, tk), :]
        vc = v_ref[pl.ds(off, tk), :]
        s = lax.dot_general(q4, kc, (((1,), (1,)), ((), ())), preferred_element_type=F32)
        m_new = jnp.maximum(m, jnp.max(s, axis=-1, keepdims=True))
        alpha = jnp.exp(m - m_new)
        p = jnp.exp(s - m_new)
        l = alpha * l + jnp.sum(p, axis=-1, keepdims=True)
        acc = alpha * acc + jnp.dot(p.astype(BF16), vc, preferred_element_type=F32)
        return m_new, l, acc

    init = (jnp.full((Q_PER_KV * tq, 1), -jnp.inf, F32), jnp.zeros((Q_PER_KV * tq, 1), F32),
            jnp.zeros((Q_PER_KV * tq, hd), F32))
    _, l, acc = lax.fori_loop(0, nkv, step, init)
    o = acc / l
    for g in range(Q_PER_KV):
        o_ref[:, g * hd:(g + 1) * hd] = o[g * tq:(g + 1) * tq].astype(o_ref.dtype)


def _attention(q, k, z1, *, q_row0, q_rows, kv_row0, kv_rows, v_col0, hd):
    aw, kvw = q.shape[1], k.shape[1]
    n_kv = kvw // hd
    gw = Q_PER_KV * hd
    tq = _pick(q_rows, (256, 128))
    tk = _pick(kv_rows, (768, 512, 256, 128))
    assert q_row0 % tq == 0 and kv_row0 % kv_rows == 0 and v_col0 % hd == 0
    qb0, kb0, vc0 = q_row0 // tq, kv_row0 // kv_rows, v_col0 // hd
    vm = 2 * (2 * tq * gw * 2 + 2 * kv_rows * hd * 2) + 6 * Q_PER_KV * tq * tk * 4
    return pl.pallas_call(
        functools.partial(_attn_kernel, hd=hd, tk=tk, nkv=kv_rows // tk),
        grid=(n_kv, q_rows // tq),
        in_specs=[
            pl.BlockSpec((tq, gw), lambda h, i: (qb0 + i, h)),
            pl.BlockSpec((kv_rows, hd), lambda h, i: (kb0, h)),
            pl.BlockSpec((kv_rows, hd), lambda h, i: (kb0, vc0 + h)),
        ],
        out_specs=pl.BlockSpec((tq, gw), lambda h, i: (i, h)),
        out_shape=jax.ShapeDtypeStruct((q_rows, aw), BF16),
        compiler_params=_cparams(("parallel", "parallel"), vm),
        name="attention",
    )(q, k, z1)


def _halo_specs(tr, hr, width, col_block, n_rows):
    per = tr // hr
    last = n_rows // hr - 1
    prev = pl.BlockSpec((hr, width), lambda i, *_: (jnp.maximum(i * per - 1, 0), col_block(*_)))
    nxt = pl.BlockSpec((hr, width), lambda i, *_: (jnp.minimum((i + 1) * per, last), col_block(*_)))
    return prev, nxt


def _edge_flags(i, tr, n_lat, n_rows):
    r0 = i * tr
    r1 = r0 + tr
    pm = jnp.where((r0 == 0) | (r0 == n_lat), 0.0, 1.0).astype(F32)
    nm = jnp.where((r1 == n_lat) | (r1 == n_rows), 0.0, 1.0).astype(F32)
    return pm, nm


def _dwconv3(x, prev_row, next_row, w):
    tr = x.shape[0]
    rid = lax.broadcasted_iota(jnp.int32, x.shape, 0)
    xp = jnp.where(rid == 0, prev_row, pltpu.roll(x, 1, 0))
    xn = jnp.where(rid == tr - 1, next_row, pltpu.roll(x, tr - 1, 0))
    return xp * w[0:1] + x * w[1:2] + xn * w[2:3]


def _conv_branch_kernel(h_ref, hp_ref, hn_ref, b_ref, c_ref, cp_ref, cn_ref, w_ref, o_ref, *, n_lat, n_rows):
    tr = h_ref.shape[0]
    pm, nm = _edge_flags(pl.program_id(0), tr, n_lat, n_rows)
    hr = hp_ref.shape[0]
    u = c_ref[...].astype(F32) * h_ref[...].astype(F32)
    up = cp_ref[hr - 1:hr, :].astype(F32) * hp_ref[hr - 1:hr, :].astype(F32) * pm
    un = cn_ref[0:1, :].astype(F32) * hn_ref[0:1, :].astype(F32) * nm
    o_ref[...] = (b_ref[...].astype(F32) * _dwconv3(u, up, un, w_ref[...])).astype(o_ref.dtype)


def _conv_branch(z1, conv_w, *, rows, col0, cw, n_lat):
    tr, hr = ROW_TILE, SUBLANES_BF16
    assert col0 % cw == 0
    cb = col0 // cw
    main = lambda o: pl.BlockSpec((tr, cw), lambda i: (i, cb + o))
    hp, hn = _halo_specs(tr, hr, cw, lambda: cb, rows)
    cp, cn = _halo_specs(tr, hr, cw, lambda: cb + 2, rows)
    vm = 2 * 4 * tr * cw * 2 + 8 * tr * cw * 4
    return pl.pallas_call(
        functools.partial(_conv_branch_kernel, n_lat=n_lat, n_rows=rows),
        grid=(rows // tr,),
        in_specs=[main(0), hp, hn, main(1), main(2), cp, cn, pl.BlockSpec((3, cw), lambda i: (0, 0))],
        out_specs=pl.BlockSpec((tr, cw), lambda i: (i, 0)),
        out_shape=jax.ShapeDtypeStruct((rows, cw), BF16),
        compiler_params=_cparams(("parallel",), vm),
        name="conv_branch",
    )(z1, z1, z1, z1, z1, z1, z1, conv_w)


def _ffn_act_kernel(v_ref, vp_ref, vn_ref, g_ref, gp_ref, gn_ref, wv_ref, wg_ref, bv_ref, bg_ref, o_ref, *,
                    n_lat, n_rows):
    tr = v_ref.shape[0]
    pm, nm = _edge_flags(pl.program_id(0), tr, n_lat, n_rows)
    hr = vp_ref.shape[0]

    def conv(x_ref, p_ref, n_ref, w_ref, b_ref):
        return _dwconv3(x_ref[...].astype(F32), p_ref[hr - 1:hr, :].astype(F32) * pm,
                        n_ref[0:1, :].astype(F32) * nm, w_ref[...]) + b_ref[...]

    val = conv(v_ref, vp_ref, vn_ref, wv_ref, bv_ref)
    gate = conv(g_ref, gp_ref, gn_ref, wg_ref, bg_ref)
    o_ref[...] = (_silu(gate) * val).astype(o_ref.dtype)


def _ffn_act(u, conv_w, conv_b, *, rows, fp, n_lat):
    tr, hr = ROW_TILE, SUBLANES_BF16
    tc = _pick(fp, (1024, 512, 256, 128))
    nb = fp // tc
    vp, vn = _halo_specs(tr, hr, tc, lambda j: j, rows)
    gp, gn = _halo_specs(tr, hr, tc, lambda j: nb + j, rows)
    wspec = lambda o: pl.BlockSpec((3, tc), lambda i, j: (0, o * nb + j))
    bspec = lambda o: pl.BlockSpec((1, tc), lambda i, j: (0, o * nb + j))
    vm = 2 * 3 * tr * tc * 2 + 10 * tr * tc * 4
    return pl.pallas_call(
        functools.partial(_ffn_act_kernel, n_lat=n_lat, n_rows=rows),
        grid=(rows // tr, nb),
        in_specs=[
            pl.BlockSpec((tr, tc), lambda i, j: (i, j)), vp, vn,
            pl.BlockSpec((tr, tc), lambda i, j: (i, nb + j)), gp, gn,
            wspec(0), wspec(1), bspec(0), bspec(1),
        ],
        out_specs=pl.BlockSpec((tr, tc), lambda i, j: (i, j)),
        out_shape=jax.ShapeDtypeStruct((rows, fp), BF16),
        compiler_params=_cparams(("parallel", "parallel"), vm),
        name="ffn_act",
    )(u, u, u, u, u, u, conv_w, conv_w, conv_b, conv_b)


def _head_sums(x, hs):
    assert hs & (hs - 1) == 0
    li = lax.broadcasted_iota(jnp.int32, (LANES, LANES), 0) & -hs
    lj = lax.broadcasted_iota(jnp.int32, (LANES, LANES), 1) & -hs
    seg = jnp.where(li == lj, 1.0, 0.0).astype(F32)
    parts = [jnp.dot(x[:, t * LANES:(t + 1) * LANES], seg, preferred_element_type=F32,
                     precision=lax.Precision.HIGHEST) for t in range(x.shape[1] // LANES)]
    return jnp.concatenate(parts, axis=1)


def _rwkv_feat_kernel(z_ref, zp_ref, zn_ref, lo_ref, cw_ref, w2_ref, w0_ref, a2_ref, a0_ref, g2_ref, kk_ref,
                      ka_ref, r_o, v_o, kn_o, ld0_o, ld1_o, kd0_o, kd1_o, b0_o, b1_o, g_o, *,
                      n_lat, n_rows, dl2, il2, hs):
    tr, w3 = z_ref.shape
    W = w3 // 3
    pm, nm = _edge_flags(pl.program_id(0), tr, n_lat, n_rows)
    hr = zp_ref.shape[0]
    rkv = _dwconv3(z_ref[...], zp_ref[hr - 1:hr, :] * pm, zn_ref[0:1, :] * nm, cw_ref[...])
    r, k, v = rkv[:, :W], rkv[:, W:2 * W], rkv[:, 2 * W:]
    lw = lo_ref[:, :dl2]
    la = lo_ref[:, dl2:dl2 + il2]
    lg = lo_ref[:, dl2 + il2:]
    wl = jnp.dot(jnp.tanh(lw).astype(BF16), w2_ref[...], preferred_element_type=F32) + w0_ref[...]
    ld = -jnp.exp(-_softplus(-wl) - 0.5)
    a = _sigmoid(jnp.dot(la.astype(BF16), a2_ref[...], preferred_element_type=F32) + a0_ref[...])
    g = jnp.dot(_sigmoid(lg).astype(BF16), g2_ref[...], preferred_element_type=F32)
    kk = k * kk_ref[...]
    kn = kk / jnp.maximum(jnp.sqrt(_head_sums(kk * kk, hs)), 1e-12)
    ka = ka_ref[...]
    r_o[...] = r
    v_o[...] = v
    kn_o[...] = kn
    g_o[...] = g
    for d, (ld_o, kd_o, b_o) in enumerate(((ld0_o, kd0_o, b0_o), (ld1_o, kd1_o, b1_o))):
        ad = a[:, d * W:(d + 1) * W]
        ld_o[...] = ld[:, d * W:(d + 1) * W]
        kd_o[...] = k * (1.0 + (ad - 1.0) * ka)
        b_o[...] = kn * ad


def _rwkv_features(z2, lo, p, *, n_lat, hs):
    R, w3 = z2.shape
    W = w3 // 3
    lwp = lo.shape[1]
    tr, hr = ROW_TILE // 2, SUBLANES_F32
    dl2, il2 = p["w2"].shape[0], p["a2"].shape[0]
    zp, zn = _halo_specs(tr, hr, w3, lambda: 0, R)
    full = lambda a: pl.BlockSpec(a.shape, lambda i: (0,) * a.ndim)
    out_spec = pl.BlockSpec((tr, W), lambda i: (i, 0))
    vm = 2 * (tr * w3 * 4 + tr * lwp * 4 + 10 * tr * W * 4) + 12 * tr * w3 * 4
    outs = pl.pallas_call(
        functools.partial(_rwkv_feat_kernel, n_lat=n_lat, n_rows=R, dl2=dl2, il2=il2, hs=hs),
        grid=(R // tr,),
        in_specs=[pl.BlockSpec((tr, w3), lambda i: (i, 0)), zp, zn, pl.BlockSpec((tr, lwp), lambda i: (i, 0)),
                  full(p["cw"]), full(p["w2"]), full(p["w0"]), full(p["a2"]), full(p["a0"]), full(p["g2"]),
                  full(p["k_k"]), full(p["k_a"])],
        out_specs=[out_spec] * 10,
        out_shape=[jax.ShapeDtypeStruct((R, W), F32)] * 10,
        compiler_params=_cparams(("parallel",), vm),
        name="rwkv_features",
    )(z2, z2, z2, lo, p["cw"], p["w2"], p["w0"], p["a2"], p["a0"], p["g2"], p["k_k"], p["k_a"])
    return dict(zip(("r", "v", "kn", "ld0", "ld1", "kd0", "kd1", "b0", "b1", "g"), outs))


def _scan_kernel(rf, vf, kf, ldf, kdf, bf, rb, vb, kb, ldb, kdb, bb, yf_o, yb_o, st_ref):
    @pl.when(pl.program_id(0) == 0)
    def _():
        st_ref[...] = jnp.zeros_like(st_ref)

    C, W = rf.shape
    C2 = 2 * C
    dot = functools.partial(jnp.dot, preferred_element_type=F32)
    dot_nt = lambda a, b: lax.dot_general(a, b, (((1,), (1,)), ((), ())), preferred_element_type=F32)
    ri = lax.broadcasted_iota(jnp.int32, (C, C), 0)
    ci = lax.broadcasted_iota(jnp.int32, (C, C), 1)
    ii = lax.broadcasted_iota(jnp.int32, (C2, C2), 0)
    jj = lax.broadcasted_iota(jnp.int32, (C2, C2), 1)
    im, jm = ii & (C - 1), jj & (C - 1)
    eye = jnp.where(ii == jj, 1.0, 0.0).astype(F32)
    same_head = (ii & -C) == (jj & -C)
    h0 = lax.broadcasted_iota(jnp.int32, (C, LANES), 1) < LANES // 2

    def stack(x):
        return jnp.concatenate([jnp.where(h0, x, 0.0), jnp.where(h0, 0.0, x)], axis=0)

    def fold(x):
        return x[:C] + x[C:]

    inst = []
    for d, (r_, v_, k_, ld_, kd_, b_, y_o) in enumerate(((rf, vf, kf, ldf, kdf, bf, yf_o),
                                                          (rb, vb, kb, ldb, kdb, bb, yb_o))):
        rev = d == 1
        ld, kd, b = ld_[...], kd_[...], b_[...]
        inc = jnp.where((ci >= ri) if rev else (ci <= ri), 1.0, 0.0).astype(F32)
        cum = jnp.dot(inc, ld, preferred_element_type=F32, precision=lax.Precision.HIGHEST)
        tot = cum[0:1, :] if rev else cum[C - 1:C, :]
        e_neg = jnp.exp(-cum)
        e_dec = jnp.exp(tot - cum)
        full = dict(kt=k_[...] * jnp.exp(cum - ld), rt=r_[...] * jnp.exp(cum), kh=kd * e_neg, bh=b * e_neg,
                    kdec=kd * e_dec, bdec=b * e_dec, v=v_[...], etot=jnp.exp(tot))
        strict = (jm > im) if rev else (jm < im)
        incl = (jm >= im) if rev else (jm <= im)
        for p in range(W // LANES):
            q = {n: a[:, p * LANES:(p + 1) * LANES] for n, a in full.items()}
            q.update(d=d, p=p, rev=rev, strict=strict, incl=incl, y_o=y_o)
            inst.append(q)

    for q in inst:
        lhs = jnp.concatenate([stack(q["kt"]), stack(q["rt"])], axis=0).astype(BF16)
        rhs = jnp.concatenate([stack(q["bh"]), stack(q["kh"])], axis=0).astype(BF16)
        q["G"] = dot_nt(lhs, rhs)
    for q in inst:
        G = q.pop("G")
        q["Lb"] = jnp.where(q["strict"], G[:C2, :C2], 0.0)
        q["LkAk"] = jnp.concatenate([jnp.where(q["strict"], G[:C2, C2:], 0.0),
                                     jnp.where(q["incl"], G[C2:, C2:], 0.0)], axis=0).astype(BF16)
        q["Ab"] = jnp.where(q["incl"], G[C2:, :C2], 0.0).astype(BF16)

    s = 1
    while s < C:
        same = (ii & -(2 * s)) == (jj & -(2 * s))
        hi_i, hi_j = (ii & s) != 0, (jj & s) != 0
        off = {False: same & hi_i & ~hi_j, True: same & ~hi_i & hi_j}
        if s == 1:
            for q in inst:
                q["T"] = eye - jnp.where(off[q["rev"]], q["Lb"], 0.0)
        else:
            for q in inst:
                q["Tb"] = q["T"].astype(BF16)
                q["TM"] = dot(q["Tb"], jnp.where(off[q["rev"]], q["Lb"], 0.0).astype(BF16))
            for q in inst:
                q["T"] = q["T"] - dot(q.pop("TM").astype(BF16), q.pop("Tb"))
        s *= 2

    for q in inst:
        S = st_ref[q["d"], q["p"]]
        q["S"] = S
        q["X"] = dot_nt(jnp.concatenate([q["kt"], q["rt"]], axis=0).astype(BF16), S.astype(BF16))
        q["LA"] = dot(q.pop("LkAk"), stack(q["v"]).astype(BF16))
    for q in inst:
        q["Us"] = dot(q.pop("T").astype(BF16), (stack(q["X"][:C]) + q["LA"][:C2]).astype(BF16))
    for q in inst:
        q["ABU"] = dot(q.pop("Ab"), q["Us"].astype(BF16))
    for q in inst:
        sl = slice(q["p"] * LANES, (q["p"] + 1) * LANES)
        q["y_o"][:, sl] = q["X"][C:] + fold(q["LA"][C2:] - q.pop("ABU"))
        vu_t = jnp.concatenate([q["v"], -fold(q["Us"])], axis=0).T.astype(BF16)
        upd = dot(vu_t, jnp.concatenate([q["kdec"], q["bdec"]], axis=0).astype(BF16))
        st_ref[q["d"], q["p"]] = q["S"] * q["etot"] + jnp.where(same_head, upd, 0.0)


def _rwkv_scan(f, *, n_lat):
    R, W = f["r"].shape
    C = CHUNK
    n, nl = R // C, n_lat // C
    nc = n - nl
    fwd = lambda s: (jnp.where(s < nc, nl + s, s - nc), 0)
    bwd = lambda s: (n - 1 - s, 0)
    fs, bs = pl.BlockSpec((C, W), fwd), pl.BlockSpec((C, W), bwd)
    vm = 2 * 14 * C * W * 4 + 2 * (W // LANES) * LANES * LANES * 4 + (24 << 20)
    return pl.pallas_call(
        _scan_kernel,
        grid=(n,),
        in_specs=[fs] * 6 + [bs] * 6,
        out_specs=[fs, bs],
        out_shape=[jax.ShapeDtypeStruct((R, W), F32)] * 2,
        scratch_shapes=[pltpu.VMEM((2, W // LANES, LANES, LANES), F32)],
        compiler_params=_cparams(("arbitrary",), vm),
        name="rwkv_scan",
    )(f["r"], f["v"], f["kn"], f["ld0"], f["kd0"], f["b0"], f["r"], f["v"], f["kn"], f["ld1"], f["kd1"], f["b1"])


def _rwkv_readout_kernel(yf_ref, yb_ref, r_ref, v_ref, kd0_ref, kd1_ref, g_ref, rk_ref, lw_ref, lb_ref, o_ref, *,
                         hs):
    y = yf_ref[...] + yb_ref[...]
    mu = _head_sums(y, hs) * (1.0 / hs)
    d = y - mu
    var = _head_sums(d * d, hs) * (1.0 / hs)
    yn = d * lax.rsqrt(var + GN_EPS) * lw_ref[...] + lb_ref[...]
    bonus = _head_sums(r_ref[...] * (kd0_ref[...] + kd1_ref[...]) * rk_ref[...], hs) * v_ref[...]
    o_ref[...] = ((yn + bonus) * g_ref[...]).astype(o_ref.dtype)


def _rwkv_readout(yf, yb, f, p, *, rows, hs):
    W = yf.shape[1]
    tr = ROW_TILE
    rs = pl.BlockSpec((tr, W), lambda i: (i, 0))
    ps = pl.BlockSpec((1, W), lambda i: (0, 0))
    vm = 2 * 8 * tr * W * 4 + 10 * tr * W * 4
    return pl.pallas_call(
        functools.partial(_rwkv_readout_kernel, hs=hs),
        grid=(rows // tr,),
        in_specs=[rs] * 7 + [ps] * 3,
        out_specs=rs,
        out_shape=jax.ShapeDtypeStruct((rows, W), BF16),
        compiler_params=_cparams(("parallel",), vm),
        name="rwkv_readout",
    )(yf, yb, f["r"], f["v"], f["kd0"], f["kd1"], f["g"], p["r_k"], p["ln_w"], p["ln_b"])


def _merge_kernel(a_ref, c_ref, r_ref, wa_ref, wc_ref, wr_ref, ga_ref, gc_ref, gr_ref, o_ref):
    ya = jnp.dot(a_ref[...], wa_ref[...], preferred_element_type=F32)
    yc = jnp.dot(c_ref[...], wc_ref[...], preferred_element_type=F32)
    yr = jnp.dot(r_ref[...], wr_ref[...], preferred_element_type=F32)
    o_ref[...] = (ga_ref[...].astype(F32) * ya + gc_ref[...].astype(F32) * yc
                  + gr_ref[...].astype(F32) * yr).astype(o_ref.dtype)


def _merge(att, cv, rw, wa, wc, wr, gates, *, rows):
    D = wa.shape[1]
    tm = _pick(rows, (512, 256, 128, 64))
    tn = _pick(D, (1024, 512, 256, 128))
    nb = D // tn
    ka, kc, kr = wa.shape[0], wc.shape[0], wr.shape[0]
    vm = 2 * ((tm + tn) * (ka + kc + kr) * 2 + 4 * tm * tn * 2) + 4 * tm * tn * 4
    return pl.pallas_call(
        _merge_kernel,
        grid=(nb, rows // tm),
        in_specs=[
            pl.BlockSpec((tm, ka), lambda j, i: (i, 0)),
            pl.BlockSpec((tm, kc), lambda j, i: (i, 0)),
            pl.BlockSpec((tm, kr), lambda j, i: (i, 0)),
            pl.BlockSpec((ka, tn), lambda j, i: (0, j)),
            pl.BlockSpec((kc, tn), lambda j, i: (0, j)),
            pl.BlockSpec((kr, tn), lambda j, i: (0, j)),
            pl.BlockSpec((tm, tn), lambda j, i: (i, j)),
            pl.BlockSpec((tm, tn), lambda j, i: (i, nb + j)),
            pl.BlockSpec((tm, tn), lambda j, i: (i, 2 * nb + j)),
        ],
        out_specs=pl.BlockSpec((tm, tn), lambda j, i: (i, j)),
        out_shape=jax.ShapeDtypeStruct((rows, D), BF16),
        compiler_params=_cparams(("parallel", "parallel"), vm),
        name="merge",
    )(att, cv, rw, wa, wc, wr, gates, gates, gates)


def _rope_tables(n_lat, n_ctx, hd):
    rows = n_lat // GRID_W
    row = jnp.repeat(jnp.arange(rows, dtype=F32), GRID_W)
    col = jnp.tile(jnp.arange(GRID_W, dtype=F32), rows)
    half = hd // 2
    inv_freq = ROPE_THETA ** (-jnp.arange(0, half, 2, dtype=F32) / half)
    ar, ac = row[:, None] * inv_freq, col[:, None] * inv_freq
    z = jnp.zeros_like(ar)
    cos = jnp.concatenate([jnp.cos(ar), jnp.cos(ar), jnp.cos(ac), jnp.cos(ac)], axis=-1)
    sa = jnp.concatenate([-jnp.sin(ar), z, -jnp.sin(ac), z], axis=-1)
    sb = jnp.concatenate([z, jnp.sin(ar), z, jnp.sin(ac)], axis=-1)
    pad = lambda t, v: jnp.concatenate([t, jnp.full((n_ctx, hd), v, F32)], axis=0)
    return pad(cos, 1.0), pad(sa, 0.0), pad(sb, 0.0)


def _block_diag2(w):
    z = jnp.zeros_like(w[0])
    return jnp.concatenate([jnp.concatenate([w[0], z], axis=1), jnp.concatenate([z, w[1]], axis=1)], axis=0)


def kernel(x, c, ctx, c_ctx, w_mod, b_mod, norm1_g, norm2_g, w_in, b_gate, q_norm_g, k_norm_g, w_attn_o, conv_w, w_conv_o, rwkv_conv_w, rwkv_w0, rwkv_w2, rwkv_a0, rwkv_a2, rwkv_g2, rwkv_k_k, rwkv_k_a, rwkv_r_k, rwkv_ln_w, rwkv_ln_b, w_rwkv_o, w_out, ffn_w_up, ffn_conv_w, ffn_conv_b, ffn_w_down, final_g):
    B, S, D = x.shape
    assert B == 1, "batch is folded away; the problem fixes BATCH = 1"
    Tc = ctx.shape[1]
    R = S + Tc
    L = w_mod.shape[0]
    hd = q_norm_g.shape[1]
    aw = w_attn_o.shape[1]
    cw = conv_w.shape[2]
    W = rwkv_w0.shape[2]
    hs = rwkv_r_k.shape[2]
    dl, il, gl = rwkv_w2.shape[2], rwkv_a2.shape[2], rwkv_g2.shape[1]
    ff = ffn_w_down.shape[1]
    kvw = (w_in.shape[2] - (aw + 3 * cw + 3 * W + 2 * dl + 2 * il + gl + 3 * D)) // 2
    assert S % ROW_TILE == 0 and Tc % ROW_TILE == 0 and S % GRID_W == 0
    assert hd == LANES and 2 * hs == LANES and 2 * CHUNK == LANES and W % LANES == 0

    o_conv = aw + 2 * kvw
    o_rkv = o_conv + 3 * cw
    o_lora = o_rkv + 3 * W
    o_gate = o_lora + 2 * dl + 2 * il + gl
    lw_real = 2 * dl + 2 * il + gl
    lwp = -(-lw_real // LANES) * LANES
    gp = lwp - 2 * dl - 2 * il
    fp = -(-ff // 1024) * 1024

    xa = jnp.concatenate([x[0], ctx[0]], axis=0)
    cond = jnp.zeros((16, D), F32).at[0].set(c[0]).at[1].set(c_ctx)
    mods = _ada_mod(cond, w_mod, b_mod)
    tabs = _rope_tables(S, Tc, hd)

    for l in range(L):
        last = l == L - 1
        rows = S if last else R
        m = mods[l, :2].reshape(2, N_MOD, 1, D)
        mod = lambda i: m[:, i]

        w1 = w_in[l, :, :o_rkv].astype(BF16)
        w2a = w_in[l, :, o_rkv:o_lora].astype(BF16)
        w2b = jnp.pad(w_in[l, :, o_lora:o_gate], ((0, 0), (0, lwp - lw_real))).astype(BF16)
        w3 = w_in[l, :, o_gate:].astype(BF16)
        padf = lambda a: jnp.pad(a, ((0, 0), (0, fp - ff)))
        w_up = jnp.concatenate([padf(ffn_w_up[l, :, :ff]), padf(ffn_w_up[l, :, ff:])], axis=1).astype(BF16)
        w_dn = jnp.pad(ffn_w_down[l], ((0, fp - ff), (0, 0))).astype(BF16)
        f_cw = jnp.concatenate([padf(ffn_conv_w[l, :, :ff]), padf(ffn_conv_w[l, :, ff:])], axis=1)
        f_cb = jnp.concatenate([padf(ffn_conv_b[l, None, :ff]), padf(ffn_conv_b[l, None, ff:])], axis=1)
        rp = {
            "cw": rwkv_conv_w[l],
            "w2": _block_diag2(rwkv_w2[l]).astype(BF16),
            "w0": rwkv_w0[l].reshape(1, 2 * W),
            "a2": _block_diag2(rwkv_a2[l]).astype(BF16),
            "a0": rwkv_a0[l].reshape(1, 2 * W),
            "g2": jnp.pad(rwkv_g2[l], ((0, gp - gl), (0, 0))).astype(BF16),
            "k_k": rwkv_k_k[l].reshape(1, W),
            "k_a": rwkv_k_a[l].reshape(1, W),
            "r_k": rwkv_r_k[l].reshape(1, W),
            "ln_w": rwkv_ln_w[l].reshape(1, W),
            "ln_b": rwkv_ln_b[l].reshape(1, W),
        }

        h = _norm_mod(xa, norm1_g[l], mod(0), mod(1), S, R)
        z1 = _matmul(h, w1, rows=R, out_dtype=BF16, name="in_proj_qkv_conv")
        z2 = _matmul(h, w2a, rows=R, out_dtype=F32, name="in_proj_rkv")
        lo = _matmul(h, w2b, rows=R, out_dtype=F32, name="in_proj_lora")
        gates = _matmul(h, w3, rows=rows, out_dtype=BF16, epi="sigmoid_bias", bias=b_gate[l], name="in_proj_gates")

        qn, kn = _qk_prep(z1, tabs, q_norm_g[l], k_norm_g[l], aw, kvw, hd)
        att = _attention(qn, kn, z1, q_row0=0, q_rows=S, kv_row0=0, kv_rows=R, v_col0=aw + kvw, hd=hd)
        if not last:
            att_c = _attention(qn, kn, z1, q_row0=S, q_rows=Tc, kv_row0=S, kv_rows=Tc, v_col0=aw + kvw, hd=hd)
            att = jnp.concatenate([att, att_c], axis=0)

        cv = _conv_branch(z1, conv_w[l], rows=rows, col0=o_conv, cw=cw, n_lat=S)

        feat = _rwkv_features(z2, lo, rp, n_lat=S, hs=hs)
        yf, yb = _rwkv_scan(feat, n_lat=S)
        rw = _rwkv_readout(yf, yb, feat, rp, rows=rows, hs=hs)

        mg = _merge(att, cv, rw, w_attn_o[l].astype(BF16), w_conv_o[l].astype(BF16), w_rwkv_o[l].astype(BF16),
                    gates, rows=rows)
        xa = _matmul(mg, w_out[l].astype(BF16), rows=rows, out_dtype=F32, epi="residual", res=xa,
                     gate2=mod(2).reshape(2, D), n_lat=S, name="out_proj")

        h2 = _norm_mod(xa, norm2_g[l], mod(3), mod(4), S, rows)
        u = _matmul(h2, w_up, rows=rows, out_dtype=BF16, name="ffn_up")
        act = _ffn_act(u, f_cw, f_cb, rows=rows, fp=fp, n_lat=S)
        xa = _matmul(act, w_dn, rows=rows, out_dtype=F32, epi="residual", res=xa, gate2=mod(5).reshape(2, D),
                     n_lat=S, name="ffn_down")

    return _final_norm(xa, final_g, S)[None]
```

```python
import functools

import jax
import jax.numpy as jnp
from jax import lax
from jax.experimental import pallas as pl
from jax.experimental.pallas import tpu as pltpu

F32 = jnp.float32
BF16 = jnp.bfloat16

GRID_W = 64
ROPE_THETA = 10000.0
Q_PER_KV = 4
N_MOD = 6
NORM_EPS = 1e-6
GN_EPS = 64e-5
LOG2_E = 1.4426950408889634

LANES = 128
SUBLANES_F32 = 8
SUBLANES_BF16 = 16
VMEM_LIMIT_CAP = 56 * 1024 * 1024
MM_VMEM_BUDGET = 48 * 1024 * 1024
CHUNK = 64
ROW_TILE = 256


def _pick(n, candidates):
    for c in candidates:
        if n % c == 0:
            return c
    raise ValueError(f"no tile in {candidates} divides {n}")


def _cparams(sem, vmem_bytes):
    limit = int(min(max(vmem_bytes * 5 // 4 + (4 << 20), 16 << 20), VMEM_LIMIT_CAP))
    return pltpu.CompilerParams(dimension_semantics=sem, vmem_limit_bytes=limit)


def _sigmoid(x):
    return 1.0 / (1.0 + jnp.exp(-x))


def _silu(x):
    return x * _sigmoid(x)


def _softplus(x):
    return jnp.maximum(x, 0.0) + jnp.log(1.0 + jnp.exp(-jnp.abs(x)))


def _mod_kernel(c_ref, w_ref, b_ref, o_ref):
    a = _silu(c_ref[...]).astype(BF16)
    o_ref[...] = jnp.dot(a, w_ref[...].astype(BF16), preferred_element_type=F32) + b_ref[...]


def _ada_mod(cond, w_mod, b_mod):
    L, D, N = w_mod.shape
    tn = _pick(N, (512, 256, 128))
    vm = 2 * (D * tn * 4 + 16 * tn * 4 * 2) + 16 * D * 4 * 2 + D * tn * 2
    return pl.pallas_call(
        _mod_kernel,
        grid=(L, N // tn),
        in_specs=[
            pl.BlockSpec((16, D), lambda l, j: (0, 0)),
            pl.BlockSpec((None, D, tn), lambda l, j: (l, 0, j)),
            pl.BlockSpec((None, 1, tn), lambda l, j: (l, 0, j)),
        ],
        out_specs=pl.BlockSpec((None, 16, tn), lambda l, j: (l, 0, j)),
        out_shape=jax.ShapeDtypeStruct((L, 16, N), F32),
        compiler_params=_cparams(("parallel", "parallel"), vm),
        name="ada_mod",
    )(cond, w_mod, b_mod.reshape(L, 1, N))


def _norm_mod_kernel(x_ref, g_ref, sh_ref, sc_ref, o_ref):
    x = x_ref[...]
    y = x * lax.rsqrt(jnp.mean(x * x, axis=-1, keepdims=True) + NORM_EPS) * g_ref[...]
    o_ref[...] = (y * (1.0 + sc_ref[...]) + sh_ref[...]).astype(o_ref.dtype)


def _norm_mod(x, g, shift2, scale2, n_lat, rows):
    D = x.shape[1]
    tr = ROW_TILE
    nl = n_lat // tr
    vm = 2 * (tr * D * 4 + tr * D * 2) + 6 * D * 4
    return pl.pallas_call(
        _norm_mod_kernel,
        grid=(rows // tr,),
        in_specs=[
            pl.BlockSpec((tr, D), lambda i: (i, 0)),
            pl.BlockSpec((1, D), lambda i: (0, 0)),
            pl.BlockSpec((None, 1, D), lambda i: (jnp.where(i >= nl, 1, 0), 0, 0)),
            pl.BlockSpec((None, 1, D), lambda i: (jnp.where(i >= nl, 1, 0), 0, 0)),
        ],
        out_specs=pl.BlockSpec((tr, D), lambda i: (i, 0)),
        out_shape=jax.ShapeDtypeStruct((rows, D), BF16),
        compiler_params=_cparams(("parallel",), vm),
        name="norm_mod",
    )(x, g.reshape(1, D), shift2, scale2)


def _final_norm_kernel(x_ref, g_ref, o_ref):
    x = x_ref[...]
    o_ref[...] = x * lax.rsqrt(jnp.mean(x * x, axis=-1, keepdims=True) + NORM_EPS) * g_ref[...]


def _final_norm(x, g, rows):
    D = x.shape[1]
    tr = ROW_TILE
    return pl.pallas_call(
        _final_norm_kernel,
        grid=(rows // tr,),
        in_specs=[pl.BlockSpec((tr, D), lambda i: (i, 0)), pl.BlockSpec((1, D), lambda i: (0, 0))],
        out_specs=pl.BlockSpec((tr, D), lambda i: (i, 0)),
        out_shape=jax.ShapeDtypeStruct((rows, D), F32),
        compiler_params=_cparams(("parallel",), 4 * tr * D * 4),
        name="final_norm",
    )(x, g.reshape(1, D))


def _mm_kernel(*refs, nk, epi, tm, n_lat):
    if nk > 1:
        acc_ref = refs[-1]
        refs = refs[:-1]
    a_ref, b_ref = refs[0], refs[1]
    o_ref = refs[-1]
    extra = refs[2:-1]
    row0 = pl.program_id(0) * tm

    def finish(acc):
        if epi == "cast":
            o_ref[...] = acc.astype(o_ref.dtype)
        elif epi == "sigmoid_bias":
            o_ref[...] = _sigmoid(acc + extra[0][...]).astype(o_ref.dtype)
        elif epi == "residual":
            res_ref, gate_ref = extra
            row = row0 + lax.broadcasted_iota(jnp.int32, acc.shape, 0)
            gate = jnp.where(row < n_lat, gate_ref[0:1, :], gate_ref[1:2, :])
            o_ref[...] = res_ref[...] + gate * acc
        else:
            raise ValueError(epi)

    part = jnp.dot(a_ref[...], b_ref[...], preferred_element_type=F32)
    if nk == 1:
        finish(part)
        return
    k = pl.program_id(2)

    @pl.when(k == 0)
    def _():
        acc_ref[...] = part

    @pl.when(k > 0)
    def _():
        acc_ref[...] += part

    @pl.when(k == nk - 1)
    def _():
        finish(acc_ref[...])


def _mm_tiles(rows, n, col0, K, osz, residual):
    best = None
    for tm in (1408, 1024, 768, 512, 256, 128, 64):
        if rows % tm:
            continue
        for tn in (2048, 1024, 512, 256, 128):
            if n % tn or col0 % tn:
                continue
            for tk in ((K,) if K <= 4096 else (2816, 2048, 1408, 1024, 512, 256, 128)):
                if K % tk:
                    continue
                vm = 2 * (tm * tk * 2 + tk * tn * 2 + tm * tn * osz) + tm * tn * 4
                vm += tm * tn * 4 if tk < K else 0
                vm += 2 * tm * tn * 4 if residual else 0
                if vm > MM_VMEM_BUDGET:
                    continue
                key = (tm * tn / (tm + tn), tk)
                if best is None or key > best[0]:
                    best = (key, (tm, tn, tk), vm)
    assert best is not None
    return best[1], best[2]


def _matmul(a, b, layer, *, rows, out_dtype, col0=0, n=None, epi="cast", bias=None, res=None, gate2=None,
            n_lat=0, name="mm"):
    K = b.shape[1]
    n = b.shape[2] - col0 if n is None else n
    (tm, tn, tk), vm = _mm_tiles(rows, n, col0, K, jnp.dtype(out_dtype).itemsize, epi == "residual")
    nk = K // tk
    cb = col0 // tn
    in_specs = [
        pl.BlockSpec((tm, tk), lambda i, j, k: (i, k)),
        pl.BlockSpec((None, tk, tn), lambda i, j, k: (layer, k, cb + j)),
    ]
    args = [a, b]
    if epi == "sigmoid_bias":
        in_specs.append(pl.BlockSpec((1, tn), lambda i, j, k: (0, j)))
        args.append(bias.reshape(1, n))
    elif epi == "residual":
        in_specs.append(pl.BlockSpec((tm, tn), lambda i, j, k: (i, j)))
        in_specs.append(pl.BlockSpec((2, tn), lambda i, j, k: (0, j)))
        args += [res, gate2]
    scratch = [pltpu.VMEM((tm, tn), F32)] if nk > 1 else []
    return pl.pallas_call(
        functools.partial(_mm_kernel, nk=nk, epi=epi, tm=tm, n_lat=n_lat),
        grid=(rows // tm, n // tn, nk),
        in_specs=in_specs,
        out_specs=pl.BlockSpec((tm, tn), lambda i, j, k: (i, j)),
        out_shape=jax.ShapeDtypeStruct((rows, n), out_dtype),
        scratch_shapes=scratch,
        compiler_params=_cparams(("parallel", "parallel", "arbitrary"), vm),
        name=name,
    )(*args)


def _qk_prep_kernel(zq_ref, zk_ref, c_ref, sa_ref, sb_ref, gq_ref, gk_ref, q_ref, k_ref, *, hd):
    cos, sa, sb = c_ref[...], sa_ref[...], sb_ref[...]

    def norm_rope(x, g):
        y = x * lax.rsqrt(jnp.mean(x * x, axis=-1, keepdims=True) + NORM_EPS) * g
        return y * cos + pltpu.roll(y, hd - hd // 4, 1) * sa + pltpu.roll(y, hd // 4, 1) * sb

    gq = gq_ref[...] * (hd ** -0.5 * LOG2_E)
    for h in range(zq_ref.shape[1] // hd):
        sl = slice(h * hd, (h + 1) * hd)
        q_ref[:, sl] = norm_rope(zq_ref[:, sl].astype(F32), gq).astype(BF16)
    gk = gk_ref[...]
    for h in range(zk_ref.shape[1] // hd):
        sl = slice(h * hd, (h + 1) * hd)
        k_ref[:, sl] = norm_rope(zk_ref[:, sl].astype(F32), gk).astype(BF16)


def _qk_prep(z1, tabs, gq, gk, aw, kvw, hd):
    R = z1.shape[0]
    tr = ROW_TILE
    assert aw % kvw == 0
    cos, sa, sb = tabs
    tab_spec = pl.BlockSpec((tr, hd), lambda i: (i, 0))
    vm = 2 * 2 * (tr * aw * 2 + tr * kvw * 2) + 6 * tr * hd * 4 + 8 * tr * hd * 4
    return pl.pallas_call(
        functools.partial(_qk_prep_kernel, hd=hd),
        grid=(R // tr,),
        in_specs=[
            pl.BlockSpec((tr, aw), lambda i: (i, 0)),
            pl.BlockSpec((tr, kvw), lambda i: (i, aw // kvw)),
            tab_spec, tab_spec, tab_spec,
            pl.BlockSpec((1, hd), lambda i: (0, 0)),
            pl.BlockSpec((1, hd), lambda i: (0, 0)),
        ],
        out_specs=[pl.BlockSpec((tr, aw), lambda i: (i, 0)), pl.BlockSpec((tr, kvw), lambda i: (i, 0))],
        out_shape=[jax.ShapeDtypeStruct((R, aw), BF16), jax.ShapeDtypeStruct((R, kvw), BF16)],
        compiler_params=_cparams(("parallel",), vm),
        name="qk_prep",
    )(z1, z1, cos, sa, sb, gq.reshape(1, hd), gk.reshape(1, hd))


def _attn_kernel(q_ref, k_ref, v_ref, o_ref, *, hd, tk, nkv):
    tq = q_ref.shape[0]
    q4 = jnp.concatenate([q_ref[:, g * hd:(g + 1) * hd] for g in range(Q_PER_KV)], axis=0)

    m = l = acc = None
    for j in range(nkv):
        kc = k_ref[j * tk:(j + 1) * tk, :]
        vc = v_ref[j * tk:(j + 1) * tk, :]
        s = lax.dot_general(q4, kc, (((1,), (1,)), ((), ())), preferred_element_type=F32)
        smax = jnp.max(s, axis=-1, keepdims=True)
        if j == 0:
            m = smax
            p = jnp.exp2(s - m)
            l = jnp.sum(p, axis=-1, keepdims=True)
            acc = jnp.dot(p.astype(BF16), vc, preferred_element_type=F32)
        else:
            m_new = jnp.maximum(m, smax)
            alpha = jnp.exp2(m - m_new)
            p = jnp.exp2(s - m_new)
            l = alpha * l + jnp.sum(p, axis=-1, keepdims=True)
            acc = alpha * acc + jnp.dot(p.astype(BF16), vc, preferred_element_type=F32)
            m = m_new
    o = acc / l
    for g in range(Q_PER_KV):
        o_ref[:, g * hd:(g + 1) * hd] = o[g * tq:(g + 1) * tq].astype(o_ref.dtype)


def _attention(q, k, z1, *, q_row0, q_rows, kv_row0, kv_rows, v_col0, hd):
    aw, kvw = q.shape[1], k.shape[1]
    n_kv = kvw // hd
    gw = Q_PER_KV * hd
    tq = _pick(q_rows, (256, 128))
    tk = _pick(kv_rows, (768, 512, 256, 128))
    assert q_row0 % tq == 0 and kv_row0 % kv_rows == 0 and v_col0 % hd == 0
    qb0, kb0, vc0 = q_row0 // tq, kv_row0 // kv_rows, v_col0 // hd
    vm = 2 * (2 * tq * gw * 2 + 2 * kv_rows * hd * 2) + 6 * Q_PER_KV * tq * tk * 4
    return pl.pallas_call(
        functools.partial(_attn_kernel, hd=hd, tk=tk, nkv=kv_rows // tk),
        grid=(n_kv, q_rows // tq),
        in_specs=[
            pl.BlockSpec((tq, gw), lambda h, i: (qb0 + i, h)),
            pl.BlockSpec((kv_rows, hd), lambda h, i: (kb0, h)),
            pl.BlockSpec((kv_rows, hd), lambda h, i: (kb0, vc0 + h)),
        ],
        out_specs=pl.BlockSpec((tq, gw), lambda h, i: (i, h)),
        out_shape=jax.ShapeDtypeStruct((q_rows, aw), BF16),
        compiler_params=_cparams(("parallel", "parallel"), vm),
        name="attention",
    )(q, k, z1)


def _halo_specs(tr, hr, width, col_block, n_rows):
    per = tr // hr
    last = n_rows // hr - 1
    prev = pl.BlockSpec((hr, width), lambda i, *_: (jnp.maximum(i * per - 1, 0), col_block(*_)))
    nxt = pl.BlockSpec((hr, width), lambda i, *_: (jnp.minimum((i + 1) * per, last), col_block(*_)))
    return prev, nxt


def _edge_flags(i, tr, n_lat, n_rows):
    r0 = i * tr
    r1 = r0 + tr
    pm = jnp.where((r0 == 0) | (r0 == n_lat), 0.0, 1.0).astype(F32)
    nm = jnp.where((r1 == n_lat) | (r1 == n_rows), 0.0, 1.0).astype(F32)
    return pm, nm


def _dwconv3(x, prev_row, next_row, w):
    tr = x.shape[0]
    rid = lax.broadcasted_iota(jnp.int32, x.shape, 0)
    xp = jnp.where(rid == 0, prev_row, pltpu.roll(x, 1, 0))
    xn = jnp.where(rid == tr - 1, next_row, pltpu.roll(x, tr - 1, 0))
    return xp * w[0:1] + x * w[1:2] + xn * w[2:3]


def _conv_branch_kernel(h_ref, hp_ref, hn_ref, b_ref, c_ref, cp_ref, cn_ref, w_ref, o_ref, *, n_lat, n_rows):
    tr = h_ref.shape[0]
    pm, nm = _edge_flags(pl.program_id(0), tr, n_lat, n_rows)
    hr = hp_ref.shape[0]
    u = c_ref[...].astype(F32) * h_ref[...].astype(F32)
    up = cp_ref[hr - 1:hr, :].astype(F32) * hp_ref[hr - 1:hr, :].astype(F32) * pm
    un = cn_ref[0:1, :].astype(F32) * hn_ref[0:1, :].astype(F32) * nm
    o_ref[...] = (b_ref[...].astype(F32) * _dwconv3(u, up, un, w_ref[...])).astype(o_ref.dtype)


def _conv_branch(z1, conv_w, *, rows, col0, cw, n_lat):
    tr, hr = ROW_TILE, SUBLANES_BF16
    assert col0 % cw == 0
    cb = col0 // cw
    main = lambda o: pl.BlockSpec((tr, cw), lambda i: (i, cb + o))
    hp, hn = _halo_specs(tr, hr, cw, lambda: cb, rows)
    cp, cn = _halo_specs(tr, hr, cw, lambda: cb + 2, rows)
    vm = 2 * 4 * tr * cw * 2 + 8 * tr * cw * 4
    return pl.pallas_call(
        functools.partial(_conv_branch_kernel, n_lat=n_lat, n_rows=rows),
        grid=(rows // tr,),
        in_specs=[main(0), hp, hn, main(1), main(2), cp, cn, pl.BlockSpec((3, cw), lambda i: (0, 0))],
        out_specs=pl.BlockSpec((tr, cw), lambda i: (i, 0)),
        out_shape=jax.ShapeDtypeStruct((rows, cw), BF16),
        compiler_params=_cparams(("parallel",), vm),
        name="conv_branch",
    )(z1, z1, z1, z1, z1, z1, z1, conv_w)


def _ffn_act_kernel(v_ref, vp_ref, vn_ref, g_ref, gp_ref, gn_ref, wv_ref, wg_ref, bv_ref, bg_ref, o_ref, *,
                    n_lat, n_rows):
    tr = v_ref.shape[0]
    pm, nm = _edge_flags(pl.program_id(0), tr, n_lat, n_rows)
    hr = vp_ref.shape[0]
    ri = lax.broadcasted_iota(jnp.int32, (2 * tr, tr), 0)
    ci = lax.broadcasted_iota(jnp.int32, (2 * tr, tr), 1)
    shift = jnp.where(ci == jnp.where(ri < tr, ri - 1, ri - tr + 1), 1.0, 0.0).astype(v_ref.dtype)
    r8 = lax.broadcasted_iota(jnp.int32, (SUBLANES_F32, v_ref.shape[1]), 0)

    def conv(x_ref, p_ref, n_ref, w_ref, b_ref):
        x = x_ref[...]
        sh = jnp.dot(shift, x, preferred_element_type=F32)
        first = sh[:SUBLANES_F32] + jnp.where(r8 == 0, p_ref[hr - 1:hr, :].astype(F32) * pm, 0.0)
        lastr = sh[2 * tr - SUBLANES_F32:] + jnp.where(r8 == SUBLANES_F32 - 1, n_ref[0:1, :].astype(F32) * nm, 0.0)
        xp = jnp.concatenate([first, sh[SUBLANES_F32:tr]], axis=0)
        xn = jnp.concatenate([sh[tr:2 * tr - SUBLANES_F32], lastr], axis=0)
        w = w_ref[...]
        return xp * w[0:1] + x.astype(F32) * w[1:2] + xn * w[2:3] + b_ref[...]

    val = conv(v_ref, vp_ref, vn_ref, wv_ref, bv_ref)
    gate = conv(g_ref, gp_ref, gn_ref, wg_ref, bg_ref)
    o_ref[...] = (_silu(gate) * val).astype(o_ref.dtype)


def _ffn_act(u, conv_w, conv_b, *, rows, fp, n_lat):
    tr, hr = ROW_TILE, SUBLANES_BF16
    tc = _pick(fp, (1024, 512, 256, 128))
    nb = fp // tc
    vp, vn = _halo_specs(tr, hr, tc, lambda j: j, rows)
    gp, gn = _halo_specs(tr, hr, tc, lambda j: nb + j, rows)
    wspec = lambda o: pl.BlockSpec((3, tc), lambda i, j: (0, o * nb + j))
    bspec = lambda o: pl.BlockSpec((1, tc), lambda i, j: (0, o * nb + j))
    vm = 2 * 3 * tr * tc * 2 + 10 * tr * tc * 4
    return pl.pallas_call(
        functools.partial(_ffn_act_kernel, n_lat=n_lat, n_rows=rows),
        grid=(rows // tr, nb),
        in_specs=[
            pl.BlockSpec((tr, tc), lambda i, j: (i, j)), vp, vn,
            pl.BlockSpec((tr, tc), lambda i, j: (i, nb + j)), gp, gn,
            wspec(0), wspec(1), bspec(0), bspec(1),
        ],
        out_specs=pl.BlockSpec((tr, tc), lambda i, j: (i, j)),
        out_shape=jax.ShapeDtypeStruct((rows, fp), BF16),
        compiler_params=_cparams(("parallel", "parallel"), vm),
        name="ffn_act",
    )(u, u, u, u, u, u, conv_w, conv_w, conv_b, conv_b)


def _head_sums(x, hs):
    assert hs & (hs - 1) == 0
    li = lax.broadcasted_iota(jnp.int32, (LANES, LANES), 0) & -hs
    lj = lax.broadcasted_iota(jnp.int32, (LANES, LANES), 1) & -hs
    seg = jnp.where(li == lj, 1.0, 0.0).astype(F32)
    parts = [jnp.dot(x[:, t * LANES:(t + 1) * LANES], seg, preferred_element_type=F32,
                     precision=lax.Precision.HIGHEST) for t in range(x.shape[1] // LANES)]
    return jnp.concatenate(parts, axis=1)


def _rwkv_feat_kernel(z_ref, zp_ref, zn_ref, lo_ref, cw_ref, w2_ref, w0_ref, a2_ref, a0_ref, g2_ref, kk_ref,
                      ka_ref, r_o, v_o, kn_o, ld0_o, ld1_o, kd0_o, kd1_o, b0_o, b1_o, g_o, *,
                      n_lat, n_rows, dl2, il2, hs):
    tr, w3 = z_ref.shape
    W = w3 // 3
    pm, nm = _edge_flags(pl.program_id(0), tr, n_lat, n_rows)
    hr = zp_ref.shape[0]
    rkv = _dwconv3(z_ref[...], zp_ref[hr - 1:hr, :] * pm, zn_ref[0:1, :] * nm, cw_ref[...])
    r, k, v = rkv[:, :W], rkv[:, W:2 * W], rkv[:, 2 * W:]
    lw = lo_ref[:, :dl2]
    la = lo_ref[:, dl2:dl2 + il2]
    lg = lo_ref[:, dl2 + il2:]
    wl = jnp.dot(jnp.tanh(lw).astype(BF16), w2_ref[...], preferred_element_type=F32) + w0_ref[...]
    ld = -jnp.exp(-_softplus(-wl) - 0.5)
    a = _sigmoid(jnp.dot(la.astype(BF16), a2_ref[...], preferred_element_type=F32) + a0_ref[...])
    g = jnp.dot(_sigmoid(lg).astype(BF16), g2_ref[...], preferred_element_type=F32)
    kk = k * kk_ref[...]
    kn = kk / jnp.maximum(jnp.sqrt(_head_sums(kk * kk, hs)), 1e-12)
    ka = ka_ref[...]
    r_o[...] = r
    v_o[...] = v
    kn_o[...] = kn
    g_o[...] = g
    for d, (ld_o, kd_o, b_o) in enumerate(((ld0_o, kd0_o, b0_o), (ld1_o, kd1_o, b1_o))):
        ad = a[:, d * W:(d + 1) * W]
        ld_o[...] = ld[:, d * W:(d + 1) * W]
        kd_o[...] = k * (1.0 + (ad - 1.0) * ka)
        b_o[...] = kn * ad


def _rwkv_features(z2, lo, p, *, n_lat, hs):
    R, w3 = z2.shape
    W = w3 // 3
    lwp = lo.shape[1]
    tr, hr = ROW_TILE // 2, SUBLANES_F32
    dl2, il2 = p["w2"].shape[0], p["a2"].shape[0]
    zp, zn = _halo_specs(tr, hr, w3, lambda: 0, R)
    full = lambda a: pl.BlockSpec(a.shape, lambda i: (0,) * a.ndim)
    out_spec = pl.BlockSpec((tr, W), lambda i: (i, 0))
    vm = 2 * (tr * w3 * 4 + tr * lwp * 4 + 10 * tr * W * 4) + 12 * tr * w3 * 4
    outs = pl.pallas_call(
        functools.partial(_rwkv_feat_kernel, n_lat=n_lat, n_rows=R, dl2=dl2, il2=il2, hs=hs),
        grid=(R // tr,),
        in_specs=[pl.BlockSpec((tr, w3), lambda i: (i, 0)), zp, zn, pl.BlockSpec((tr, lwp), lambda i: (i, 0)),
                  full(p["cw"]), full(p["w2"]), full(p["w0"]), full(p["a2"]), full(p["a0"]), full(p["g2"]),
                  full(p["k_k"]), full(p["k_a"])],
        out_specs=[out_spec] * 10,
        out_shape=[jax.ShapeDtypeStruct((R, W), F32)] * 10,
        compiler_params=_cparams(("parallel",), vm),
        name="rwkv_features",
    )(z2, z2, z2, lo, p["cw"], p["w2"], p["w0"], p["a2"], p["a0"], p["g2"], p["k_k"], p["k_a"])
    return dict(zip(("r", "v", "kn", "ld0", "ld1", "kd0", "kd1", "b0", "b1", "g"), outs))


def _scan_kernel(rf, vf, kf, ldf, kdf, bf, rb, vb, kb, ldb, kdb, bb, yf_o, yb_o, st_ref):
    @pl.when(pl.program_id(0) == 0)
    def _():
        st_ref[...] = jnp.zeros_like(st_ref)

    C, W = rf.shape
    C2 = 2 * C
    dot = functools.partial(jnp.dot, preferred_element_type=F32)
    dot_nt = lambda a, b: lax.dot_general(a, b, (((1,), (1,)), ((), ())), preferred_element_type=F32)
    ri = lax.broadcasted_iota(jnp.int32, (C, C), 0)
    ci = lax.broadcasted_iota(jnp.int32, (C, C), 1)
    ii = lax.broadcasted_iota(jnp.int32, (C2, C2), 0)
    jj = lax.broadcasted_iota(jnp.int32, (C2, C2), 1)
    im, jm = ii & (C - 1), jj & (C - 1)
    eye = jnp.where(ii == jj, 1.0, 0.0).astype(F32)
    same_head = (ii & -C) == (jj & -C)
    h0 = lax.broadcasted_iota(jnp.int32, (C, LANES), 1) < LANES // 2

    def stack(x):
        return jnp.concatenate([jnp.where(h0, x, 0.0), jnp.where(h0, 0.0, x)], axis=0)

    def fold(x):
        return x[:C] + x[C:]

    inst = []
    for d, (r_, v_, k_, ld_, kd_, b_, y_o) in enumerate(((rf, vf, kf, ldf, kdf, bf, yf_o),
                                                          (rb, vb, kb, ldb, kdb, bb, yb_o))):
        rev = d == 1
        ld, kd, b = ld_[...], kd_[...], b_[...]
        inc = jnp.where((ci >= ri) if rev else (ci <= ri), 1.0, 0.0).astype(F32)
        cum = jnp.dot(inc, ld, preferred_element_type=F32, precision=lax.Precision.HIGHEST)
        tot = cum[0:1, :] if rev else cum[C - 1:C, :]
        e_neg = jnp.exp(-cum)
        e_dec = jnp.exp(tot - cum)
        full = dict(kt=k_[...] * jnp.exp(cum - ld), rt=r_[...] * jnp.exp(cum), kh=kd * e_neg, bh=b * e_neg,
                    kdec=kd * e_dec, bdec=b * e_dec, v=v_[...], etot=jnp.exp(tot))
        strict = (jm > im) if rev else (jm < im)
        incl = (jm >= im) if rev else (jm <= im)
        for p in range(W // LANES):
            q = {n: a[:, p * LANES:(p + 1) * LANES] for n, a in full.items()}
            q.update(d=d, p=p, rev=rev, strict=strict, incl=incl, y_o=y_o)
            inst.append(q)

    for q in inst:
        lhs = jnp.concatenate([stack(q["kt"]), stack(q["rt"])], axis=0).astype(BF16)
        rhs = jnp.concatenate([stack(q["bh"]), stack(q["kh"])], axis=0).astype(BF16)
        q["G"] = dot_nt(lhs, rhs)
    for q in inst:
        G = q.pop("G")
        q["Lb"] = jnp.where(q["strict"], G[:C2, :C2], 0.0)
        q["LkAk"] = jnp.concatenate([jnp.where(q["strict"], G[:C2, C2:], 0.0),
                                     jnp.where(q["incl"], G[C2:, C2:], 0.0)], axis=0).astype(BF16)
        q["Ab"] = jnp.where(q["incl"], G[C2:, :C2], 0.0).astype(BF16)

    s = 1
    while s < C:
        same = (ii & -(2 * s)) == (jj & -(2 * s))
        hi_i, hi_j = (ii & s) != 0, (jj & s) != 0
        off = {False: same & hi_i & ~hi_j, True: same & ~hi_i & hi_j}
        if s == 1:
            for q in inst:
                q["T"] = eye - jnp.where(off[q["rev"]], q["Lb"], 0.0)
        else:
            for q in inst:
                q["Tb"] = q["T"].astype(BF16)
                q["TM"] = dot(q["Tb"], jnp.where(off[q["rev"]], q["Lb"], 0.0).astype(BF16))
            for q in inst:
                q["T"] = q["T"] - dot(q.pop("TM").astype(BF16), q.pop("Tb"))
        s *= 2

    for q in inst:
        S = st_ref[q["d"], q["p"]]
        q["S"] = S
        q["X"] = dot_nt(jnp.concatenate([q["kt"], q["rt"]], axis=0).astype(BF16), S.astype(BF16))
        q["LA"] = dot(q.pop("LkAk"), stack(q["v"]).astype(BF16))
    for q in inst:
        q["Us"] = dot(q.pop("T").astype(BF16), (stack(q["X"][:C]) + q["LA"][:C2]).astype(BF16))
    for q in inst:
        q["ABU"] = dot(q.pop("Ab"), q["Us"].astype(BF16))
    for q in inst:
        sl = slice(q["p"] * LANES, (q["p"] + 1) * LANES)
        q["y_o"][:, sl] = q["X"][C:] + fold(q["LA"][C2:] - q.pop("ABU"))
        vu_t = jnp.concatenate([q["v"], -fold(q["Us"])], axis=0).T.astype(BF16)
        upd = dot(vu_t, jnp.concatenate([q["kdec"], q["bdec"]], axis=0).astype(BF16))
        st_ref[q["d"], q["p"]] = q["S"] * q["etot"] + jnp.where(same_head, upd, 0.0)


def _rwkv_scan(f, *, n_lat):
    R, W = f["r"].shape
    C = CHUNK
    n, nl = R // C, n_lat // C
    nc = n - nl
    fwd = lambda s: (jnp.where(s < nc, nl + s, s - nc), 0)
    bwd = lambda s: (n - 1 - s, 0)
    fs, bs = pl.BlockSpec((C, W), fwd), pl.BlockSpec((C, W), bwd)
    vm = 2 * 14 * C * W * 4 + 2 * (W // LANES) * LANES * LANES * 4 + (24 << 20)
    return pl.pallas_call(
        _scan_kernel,
        grid=(n,),
        in_specs=[fs] * 6 + [bs] * 6,
        out_specs=[fs, bs],
        out_shape=[jax.ShapeDtypeStruct((R, W), F32)] * 2,
        scratch_shapes=[pltpu.VMEM((2, W // LANES, LANES, LANES), F32)],
        compiler_params=_cparams(("arbitrary",), vm),
        name="rwkv_scan",
    )(f["r"], f["v"], f["kn"], f["ld0"], f["kd0"], f["b0"], f["r"], f["v"], f["kn"], f["ld1"], f["kd1"], f["b1"])


def _rwkv_readout_kernel(yf_ref, yb_ref, r_ref, v_ref, kd0_ref, kd1_ref, g_ref, rk_ref, lw_ref, lb_ref, o_ref, *,
                         hs):
    y = yf_ref[...] + yb_ref[...]
    mu = _head_sums(y, hs) * (1.0 / hs)
    d = y - mu
    var = _head_sums(d * d, hs) * (1.0 / hs)
    yn = d * lax.rsqrt(var + GN_EPS) * lw_ref[...] + lb_ref[...]
    bonus = _head_sums(r_ref[...] * (kd0_ref[...] + kd1_ref[...]) * rk_ref[...], hs) * v_ref[...]
    o_ref[...] = ((yn + bonus) * g_ref[...]).astype(o_ref.dtype)


def _rwkv_readout(yf, yb, f, p, *, rows, hs):
    W = yf.shape[1]
    tr = ROW_TILE
    rs = pl.BlockSpec((tr, W), lambda i: (i, 0))
    ps = pl.BlockSpec((1, W), lambda i: (0, 0))
    vm = 2 * 8 * tr * W * 4 + 10 * tr * W * 4
    return pl.pallas_call(
        functools.partial(_rwkv_readout_kernel, hs=hs),
        grid=(rows // tr,),
        in_specs=[rs] * 7 + [ps] * 3,
        out_specs=rs,
        out_shape=jax.ShapeDtypeStruct((rows, W), BF16),
        compiler_params=_cparams(("parallel",), vm),
        name="rwkv_readout",
    )(yf, yb, f["r"], f["v"], f["kd0"], f["kd1"], f["g"], p["r_k"], p["ln_w"], p["ln_b"])


def _merge_kernel(a_ref, c_ref, r_ref, wa_ref, wc_ref, wr_ref, ga_ref, gc_ref, gr_ref, o_ref):
    ya = jnp.dot(a_ref[...], wa_ref[...], preferred_element_type=F32)
    yc = jnp.dot(c_ref[...], wc_ref[...], preferred_element_type=F32)
    yr = jnp.dot(r_ref[...], wr_ref[...], preferred_element_type=F32)
    o_ref[...] = (ga_ref[...].astype(F32) * ya + gc_ref[...].astype(F32) * yc
                  + gr_ref[...].astype(F32) * yr).astype(o_ref.dtype)


def _merge(att, cv, rw, wa, wc, wr, layer, gates, *, rows):
    D = wa.shape[2]
    tm = _pick(rows, (512, 256, 128, 64))
    tn = _pick(D, (1024, 512, 256, 128))
    nb = D // tn
    ka, kc, kr = wa.shape[1], wc.shape[1], wr.shape[1]
    vm = 2 * ((tm + tn) * (ka + kc + kr) * 2 + 4 * tm * tn * 2) + 4 * tm * tn * 4
    return pl.pallas_call(
        _merge_kernel,
        grid=(nb, rows // tm),
        in_specs=[
            pl.BlockSpec((tm, ka), lambda j, i: (i, 0)),
            pl.BlockSpec((tm, kc), lambda j, i: (i, 0)),
            pl.BlockSpec((tm, kr), lambda j, i: (i, 0)),
            pl.BlockSpec((None, ka, tn), lambda j, i: (layer, 0, j)),
            pl.BlockSpec((None, kc, tn), lambda j, i: (layer, 0, j)),
            pl.BlockSpec((None, kr, tn), lambda j, i: (layer, 0, j)),
            pl.BlockSpec((tm, tn), lambda j, i: (i, j)),
            pl.BlockSpec((tm, tn), lambda j, i: (i, nb + j)),
            pl.BlockSpec((tm, tn), lambda j, i: (i, 2 * nb + j)),
        ],
        out_specs=pl.BlockSpec((tm, tn), lambda j, i: (i, j)),
        out_shape=jax.ShapeDtypeStruct((rows, D), BF16),
        compiler_params=_cparams(("parallel", "parallel"), vm),
        name="merge",
    )(att, cv, rw, wa, wc, wr, gates, gates, gates)


def _rope_tables(n_lat, n_ctx, hd):
    rows = n_lat // GRID_W
    row = jnp.repeat(jnp.arange(rows, dtype=F32), GRID_W)
    col = jnp.tile(jnp.arange(GRID_W, dtype=F32), rows)
    half = hd // 2
    inv_freq = ROPE_THETA ** (-jnp.arange(0, half, 2, dtype=F32) / half)
    ar, ac = row[:, None] * inv_freq, col[:, None] * inv_freq
    z = jnp.zeros_like(ar)
    cos = jnp.concatenate([jnp.cos(ar), jnp.cos(ar), jnp.cos(ac), jnp.cos(ac)], axis=-1)
    sa = jnp.concatenate([-jnp.sin(ar), z, -jnp.sin(ac), z], axis=-1)
    sb = jnp.concatenate([z, jnp.sin(ar), z, jnp.sin(ac)], axis=-1)
    pad = lambda t, v: jnp.concatenate([t, jnp.full((n_ctx, hd), v, F32)], axis=0)
    return pad(cos, 1.0), pad(sa, 0.0), pad(sb, 0.0)


def _block_diag2(w):
    z = jnp.zeros_like(w[0])
    return jnp.concatenate([jnp.concatenate([w[0], z], axis=1), jnp.concatenate([z, w[1]], axis=1)], axis=0)


def kernel(x, c, ctx, c_ctx, w_mod, b_mod, norm1_g, norm2_g, w_in, b_gate, q_norm_g, k_norm_g, w_attn_o, conv_w, w_conv_o, rwkv_conv_w, rwkv_w0, rwkv_w2, rwkv_a0, rwkv_a2, rwkv_g2, rwkv_k_k, rwkv_k_a, rwkv_r_k, rwkv_ln_w, rwkv_ln_b, w_rwkv_o, w_out, ffn_w_up, ffn_conv_w, ffn_conv_b, ffn_w_down, final_g):
    B, S, D = x.shape
    assert B == 1, "batch is folded away; the problem fixes BATCH = 1"
    Tc = ctx.shape[1]
    R = S + Tc
    L = w_mod.shape[0]
    hd = q_norm_g.shape[1]
    aw = w_attn_o.shape[1]
    cw = conv_w.shape[2]
    W = rwkv_w0.shape[2]
    hs = rwkv_r_k.shape[2]
    dl, il, gl = rwkv_w2.shape[2], rwkv_a2.shape[2], rwkv_g2.shape[1]
    ff = ffn_w_down.shape[1]
    kvw = (w_in.shape[2] - (aw + 3 * cw + 3 * W + 2 * dl + 2 * il + gl + 3 * D)) // 2
    assert S % ROW_TILE == 0 and Tc % ROW_TILE == 0 and S % GRID_W == 0
    assert hd == LANES and 2 * hs == LANES and 2 * CHUNK == LANES and W % LANES == 0

    o_conv = aw + 2 * kvw
    o_rkv = o_conv + 3 * cw
    o_lora = o_rkv + 3 * W
    o_gate = o_lora + 2 * dl + 2 * il + gl
    lw_real = 2 * dl + 2 * il + gl
    lwp = -(-lw_real // LANES) * LANES
    gp = lwp - 2 * dl - 2 * il
    fp = -(-ff // 1024) * 1024

    xa = jnp.concatenate([x[0], ctx[0]], axis=0)
    cond = jnp.zeros((16, D), F32).at[0].set(c[0]).at[1].set(c_ctx)
    mods = _ada_mod(cond, w_mod, b_mod)
    tabs = _rope_tables(S, Tc, hd)

    w_a = w_in[:, :, :o_lora + lwp].astype(BF16)
    w_g = w_in[:, :, o_gate:].astype(BF16)
    w_up = jnp.pad(ffn_w_up.reshape(L, D, 2, ff), ((0, 0), (0, 0), (0, 0), (0, fp - ff))).astype(BF16)
    w_up = w_up.reshape(L, D, 2 * fp)
    w_dn = jnp.pad(ffn_w_down, ((0, 0), (0, fp - ff), (0, 0))).astype(BF16)
    w_ao, w_co, w_ro, w_o = (w.astype(BF16) for w in (w_attn_o, w_conv_o, w_rwkv_o, w_out))
    f_cw = jnp.pad(ffn_conv_w.reshape(L, 3, 2, ff), ((0, 0), (0, 0), (0, 0), (0, fp - ff))).reshape(L, 3, 2 * fp)
    f_cb = jnp.pad(ffn_conv_b.reshape(L, 1, 2, ff), ((0, 0), (0, 0), (0, 0), (0, fp - ff))).reshape(L, 1, 2 * fp)

    for l in range(L):
        last = l == L - 1
        rows = S if last else R
        m = mods[l, :2].reshape(2, N_MOD, 1, D)
        mod = lambda i: m[:, i]
        rp = {
            "cw": rwkv_conv_w[l],
            "w2": _block_diag2(rwkv_w2[l]).astype(BF16),
            "w0": rwkv_w0[l].reshape(1, 2 * W),
            "a2": _block_diag2(rwkv_a2[l]).astype(BF16),
            "a0": rwkv_a0[l].reshape(1, 2 * W),
            "g2": jnp.pad(rwkv_g2[l], ((0, gp - gl), (0, 0))).astype(BF16),
            "k_k": rwkv_k_k[l].reshape(1, W),
            "k_a": rwkv_k_a[l].reshape(1, W),
            "r_k": rwkv_r_k[l].reshape(1, W),
            "ln_w": rwkv_ln_w[l].reshape(1, W),
            "ln_b": rwkv_ln_b[l].reshape(1, W),
        }

        h = _norm_mod(xa, norm1_g[l], mod(0), mod(1), S, R)
        z1 = _matmul(h, w_a, l, rows=R, out_dtype=BF16, col0=0, n=o_rkv, name="in_proj_qkv_conv")
        z2 = _matmul(h, w_a, l, rows=R, out_dtype=F32, col0=o_rkv, n=3 * W, name="in_proj_rkv")
        lo = _matmul(h, w_a, l, rows=R, out_dtype=F32, col0=o_lora, n=lwp, name="in_proj_lora")
        gates = _matmul(h, w_g, l, rows=rows, out_dtype=BF16, epi="sigmoid_bias", bias=b_gate[l],
                        name="in_proj_gates")

        qn, kn = _qk_prep(z1, tabs, q_norm_g[l], k_norm_g[l], aw, kvw, hd)
        att = _attention(qn, kn, z1, q_row0=0, q_rows=S, kv_row0=0, kv_rows=R, v_col0=aw + kvw, hd=hd)
        if not last:
            att_c = _attention(qn, kn, z1, q_row0=S, q_rows=Tc, kv_row0=S, kv_rows=Tc, v_col0=aw + kvw, hd=hd)
            att = jnp.concatenate([att, att_c], axis=0)

        cv = _conv_branch(z1, conv_w[l], rows=rows, col0=o_conv, cw=cw, n_lat=S)

        feat = _rwkv_features(z2, lo, rp, n_lat=S, hs=hs)
        yf, yb = _rwkv_scan(feat, n_lat=S)
        rw = _rwkv_readout(yf, yb, feat, rp, rows=rows, hs=hs)

        mg = _merge(att, cv, rw, w_ao, w_co, w_ro, l, gates, rows=rows)
        xa = _matmul(mg, w_o, l, rows=rows, out_dtype=F32, epi="residual", res=xa, gate2=mod(2).reshape(2, D),
                     n_lat=S, name="out_proj")

        h2 = _norm_mod(xa, norm2_g[l], mod(3), mod(4), S, rows)
        u = _matmul(h2, w_up, l, rows=rows, out_dtype=BF16, name="ffn_up")
        act = _ffn_act(u, f_cw[l], f_cb[l], rows=rows, fp=fp, n_lat=S)
        xa = _matmul(act, w_dn, l, rows=rows, out_dtype=F32, epi="residual", res=xa, gate2=mod(5).reshape(2, D),
                     n_lat=S, name="ffn_down")

    return _final_norm(xa, final_g, S)[None]
```

```python
import functools

import jax
import jax.numpy as jnp
from jax import lax
from jax.experimental import pallas as pl
from jax.experimental.pallas import tpu as pltpu

F32 = jnp.float32
BF16 = jnp.bfloat16

GRID_W = 64
ROPE_THETA = 10000.0
Q_PER_KV = 4
N_MOD = 6
NORM_EPS = 1e-6
GN_EPS = 64e-5
LOG2_E = 1.4426950408889634

LANES = 128
SUBLANES_F32 = 8
SUBLANES_BF16 = 16
VMEM_LIMIT_CAP = 56 * 1024 * 1024
MM_VMEM_BUDGET = 48 * 1024 * 1024
CHUNK = 64
ROW_TILE = 256


def _pick(n, candidates):
    for c in candidates:
        if n % c == 0:
            return c
    raise ValueError(f"no tile in {candidates} divides {n}")


def _cparams(sem, vmem_bytes):
    limit = int(min(max(vmem_bytes * 5 // 4 + (4 << 20), 16 << 20), VMEM_LIMIT_CAP))
    return pltpu.CompilerParams(dimension_semantics=sem, vmem_limit_bytes=limit)


def _sigmoid(x):
    return 1.0 / (1.0 + jnp.exp(-x))


def _silu(x):
    return x * _sigmoid(x)


def _softplus(x):
    return jnp.maximum(x, 0.0) + jnp.log(1.0 + jnp.exp(-jnp.abs(x)))


def _mod_kernel(c_ref, w_ref, b_ref, o_ref):
    a = _silu(c_ref[...]).astype(BF16)
    o_ref[...] = jnp.dot(a, w_ref[...].astype(BF16), preferred_element_type=F32) + b_ref[...]


def _ada_mod(cond, w_mod, b_mod):
    L, D, N = w_mod.shape
    tn = _pick(N, (512, 256, 128))
    vm = 2 * (D * tn * 4 + 16 * tn * 4 * 2) + 16 * D * 4 * 2 + D * tn * 2
    return pl.pallas_call(
        _mod_kernel,
        grid=(L, N // tn),
        in_specs=[
            pl.BlockSpec((16, D), lambda l, j: (0, 0)),
            pl.BlockSpec((None, D, tn), lambda l, j: (l, 0, j)),
            pl.BlockSpec((None, 1, tn), lambda l, j: (l, 0, j)),
        ],
        out_specs=pl.BlockSpec((None, 16, tn), lambda l, j: (l, 0, j)),
        out_shape=jax.ShapeDtypeStruct((L, 16, N), F32),
        compiler_params=_cparams(("parallel", "parallel"), vm),
        name="ada_mod",
    )(cond, w_mod, b_mod.reshape(L, 1, N))


def _norm_mod_kernel(x_ref, g_ref, sh_ref, sc_ref, o_ref):
    x = x_ref[...]
    y = x * lax.rsqrt(jnp.mean(x * x, axis=-1, keepdims=True) + NORM_EPS) * g_ref[...]
    o_ref[...] = (y * (1.0 + sc_ref[...]) + sh_ref[...]).astype(o_ref.dtype)


def _norm_mod(x, g, shift2, scale2, n_lat, rows):
    D = x.shape[1]
    tr = ROW_TILE
    nl = n_lat // tr
    vm = 2 * (tr * D * 4 + tr * D * 2) + 6 * D * 4
    return pl.pallas_call(
        _norm_mod_kernel,
        grid=(rows // tr,),
        in_specs=[
            pl.BlockSpec((tr, D), lambda i: (i, 0)),
            pl.BlockSpec((1, D), lambda i: (0, 0)),
            pl.BlockSpec((None, 1, D), lambda i: (jnp.where(i >= nl, 1, 0), 0, 0)),
            pl.BlockSpec((None, 1, D), lambda i: (jnp.where(i >= nl, 1, 0), 0, 0)),
        ],
        out_specs=pl.BlockSpec((tr, D), lambda i: (i, 0)),
        out_shape=jax.ShapeDtypeStruct((rows, D), BF16),
        compiler_params=_cparams(("parallel",), vm),
        name="norm_mod",
    )(x, g.reshape(1, D), shift2, scale2)


def _final_norm_kernel(x_ref, g_ref, o_ref):
    x = x_ref[...]
    o_ref[...] = x * lax.rsqrt(jnp.mean(x * x, axis=-1, keepdims=True) + NORM_EPS) * g_ref[...]


def _final_norm(x, g, rows):
    D = x.shape[1]
    tr = ROW_TILE
    return pl.pallas_call(
        _final_norm_kernel,
        grid=(rows // tr,),
        in_specs=[pl.BlockSpec((tr, D), lambda i: (i, 0)), pl.BlockSpec((1, D), lambda i: (0, 0))],
        out_specs=pl.BlockSpec((tr, D), lambda i: (i, 0)),
        out_shape=jax.ShapeDtypeStruct((rows, D), F32),
        compiler_params=_cparams(("parallel",), 4 * tr * D * 4),
        name="final_norm",
    )(x, g.reshape(1, D))


def _mm_kernel(*refs, nk, epi, tm, n_lat):
    if nk > 1:
        acc_ref = refs[-1]
        refs = refs[:-1]
    a_ref, b_ref = refs[0], refs[1]
    o_ref = refs[-1]
    extra = refs[2:-1]
    row0 = pl.program_id(0) * tm

    def finish(acc):
        if epi == "cast":
            o_ref[...] = acc.astype(o_ref.dtype)
        elif epi == "sigmoid_bias":
            o_ref[...] = _sigmoid(acc + extra[0][...]).astype(o_ref.dtype)
        elif epi == "residual":
            res_ref, gate_ref = extra
            row = row0 + lax.broadcasted_iota(jnp.int32, acc.shape, 0)
            gate = jnp.where(row < n_lat, gate_ref[0:1, :], gate_ref[1:2, :])
            o_ref[...] = res_ref[...] + gate * acc
        else:
            raise ValueError(epi)

    part = jnp.dot(a_ref[...], b_ref[...], preferred_element_type=F32)
    if nk == 1:
        finish(part)
        return
    k = pl.program_id(2)

    @pl.when(k == 0)
    def _():
        acc_ref[...] = part

    @pl.when(k > 0)
    def _():
        acc_ref[...] += part

    @pl.when(k == nk - 1)
    def _():
        finish(acc_ref[...])


def _mm_tiles(rows, n, col0, K, osz, residual):
    best = None
    for tm in (1408, 1024, 768, 512, 256, 128, 64):
        if rows % tm:
            continue
        for tn in (2048, 1024, 512, 256, 128):
            if n % tn or col0 % tn:
                continue
            for tk in ((K,) if K <= 4096 else range(LANES, 5632 + 1, LANES)):
                if K % tk:
                    continue
                vm = 2 * (tm * tk * 2 + tk * tn * 2 + tm * tn * osz) + tm * tn * 4
                vm += tm * tn * 4 if tk < K else 0
                vm += 2 * tm * tn * 4 if residual else 0
                if vm > MM_VMEM_BUDGET:
                    continue
                key = (tm * tn / (tm + tn), tk)
                if best is None or key > best[0]:
                    best = (key, (tm, tn, tk), vm)
    assert best is not None
    return best[1], best[2]


def _matmul(a, b, layer, *, rows, out_dtype, col0=0, n=None, epi="cast", bias=None, res=None, gate2=None,
            n_lat=0, name="mm"):
    K = b.shape[1]
    n = b.shape[2] - col0 if n is None else n
    (tm, tn, tk), vm = _mm_tiles(rows, n, col0, K, jnp.dtype(out_dtype).itemsize, epi == "residual")
    nk = K // tk
    cb = col0 // tn
    in_specs = [
        pl.BlockSpec((tm, tk), lambda i, j, k: (i, k)),
        pl.BlockSpec((None, tk, tn), lambda i, j, k: (layer, k, cb + j)),
    ]
    args = [a, b]
    if epi == "sigmoid_bias":
        in_specs.append(pl.BlockSpec((1, tn), lambda i, j, k: (0, j)))
        args.append(bias.reshape(1, n))
    elif epi == "residual":
        in_specs.append(pl.BlockSpec((tm, tn), lambda i, j, k: (i, j)))
        in_specs.append(pl.BlockSpec((2, tn), lambda i, j, k: (0, j)))
        args += [res, gate2]
    scratch = [pltpu.VMEM((tm, tn), F32)] if nk > 1 else []
    return pl.pallas_call(
        functools.partial(_mm_kernel, nk=nk, epi=epi, tm=tm, n_lat=n_lat),
        grid=(rows // tm, n // tn, nk),
        in_specs=in_specs,
        out_specs=pl.BlockSpec((tm, tn), lambda i, j, k: (i, j)),
        out_shape=jax.ShapeDtypeStruct((rows, n), out_dtype),
        scratch_shapes=scratch,
        compiler_params=_cparams(("parallel", "parallel", "arbitrary"), vm),
        name=name,
    )(*args)


def _qk_prep_kernel(zq_ref, zk_ref, c_ref, sa_ref, sb_ref, gq_ref, gk_ref, q_ref, k_ref, *, hd):
    cos, sa, sb = c_ref[...], sa_ref[...], sb_ref[...]

    def norm_rope(x, g):
        y = x * lax.rsqrt(jnp.mean(x * x, axis=-1, keepdims=True) + NORM_EPS) * g
        return y * cos + pltpu.roll(y, hd - hd // 4, 1) * sa + pltpu.roll(y, hd // 4, 1) * sb

    gq = gq_ref[...] * (hd ** -0.5 * LOG2_E)
    for h in range(zq_ref.shape[1] // hd):
        sl = slice(h * hd, (h + 1) * hd)
        q_ref[:, sl] = norm_rope(zq_ref[:, sl].astype(F32), gq).astype(BF16)
    gk = gk_ref[...]
    for h in range(zk_ref.shape[1] // hd):
        sl = slice(h * hd, (h + 1) * hd)
        k_ref[:, sl] = norm_rope(zk_ref[:, sl].astype(F32), gk).astype(BF16)


def _qk_prep(z1, tabs, gq, gk, aw, kvw, hd):
    R = z1.shape[0]
    tr = ROW_TILE
    assert aw % kvw == 0
    cos, sa, sb = tabs
    tab_spec = pl.BlockSpec((tr, hd), lambda i: (i, 0))
    vm = 2 * 2 * (tr * aw * 2 + tr * kvw * 2) + 6 * tr * hd * 4 + 8 * tr * hd * 4
    return pl.pallas_call(
        functools.partial(_qk_prep_kernel, hd=hd),
        grid=(R // tr,),
        in_specs=[
            pl.BlockSpec((tr, aw), lambda i: (i, 0)),
            pl.BlockSpec((tr, kvw), lambda i: (i, aw // kvw)),
            tab_spec, tab_spec, tab_spec,
            pl.BlockSpec((1, hd), lambda i: (0, 0)),
            pl.BlockSpec((1, hd), lambda i: (0, 0)),
        ],
        out_specs=[pl.BlockSpec((tr, aw), lambda i: (i, 0)), pl.BlockSpec((tr, kvw), lambda i: (i, 0))],
        out_shape=[jax.ShapeDtypeStruct((R, aw), BF16), jax.ShapeDtypeStruct((R, kvw), BF16)],
        compiler_params=_cparams(("parallel",), vm),
        name="qk_prep",
    )(z1, z1, cos, sa, sb, gq.reshape(1, hd), gk.reshape(1, hd))


def _attn_kernel(q_ref, k_ref, v_ref, o_ref, *, hd, tk, nkv):
    tq = q_ref.shape[0]
    q4 = jnp.concatenate([q_ref[:, g * hd:(g + 1) * hd] for g in range(Q_PER_KV)], axis=0)

    m = l = acc = None
    for j in range(nkv):
        kc = k_ref[j * tk:(j + 1) * tk, :]
        vc = v_ref[j * tk:(j + 1) * tk, :]
        s = lax.dot_general(q4, kc, (((1,), (1,)), ((), ())), preferred_element_type=F32)
        smax = jnp.max(s, axis=-1, keepdims=True)
        if j == 0:
            m = smax
            p = jnp.exp2(s - m)
            l = jnp.sum(p, axis=-1, keepdims=True)
            acc = jnp.dot(p.astype(BF16), vc, preferred_element_type=F32)
        else:
            m_new = jnp.maximum(m, smax)
            alpha = jnp.exp2(m - m_new)
            p = jnp.exp2(s - m_new)
            l = alpha * l + jnp.sum(p, axis=-1, keepdims=True)
            acc = alpha * acc + jnp.dot(p.astype(BF16), vc, preferred_element_type=F32)
            m = m_new
    o = acc / l
    for g in range(Q_PER_KV):
        o_ref[:, g * hd:(g + 1) * hd] = o[g * tq:(g + 1) * tq].astype(o_ref.dtype)


def _attention(q, k, z1, *, q_row0, q_rows, kv_row0, kv_rows, v_col0, hd):
    aw, kvw = q.shape[1], k.shape[1]
    n_kv = kvw // hd
    gw = Q_PER_KV * hd
    tq = _pick(q_rows, (256, 128))
    tk = _pick(kv_rows, (768, 512, 256, 128))
    assert q_row0 % tq == 0 and kv_row0 % kv_rows == 0 and v_col0 % hd == 0
    qb0, kb0, vc0 = q_row0 // tq, kv_row0 // kv_rows, v_col0 // hd
    vm = 2 * (2 * tq * gw * 2 + 2 * kv_rows * hd * 2) + 6 * Q_PER_KV * tq * tk * 4
    return pl.pallas_call(
        functools.partial(_attn_kernel, hd=hd, tk=tk, nkv=kv_rows // tk),
        grid=(n_kv, q_rows // tq),
        in_specs=[
            pl.BlockSpec((tq, gw), lambda h, i: (qb0 + i, h)),
            pl.BlockSpec((kv_rows, hd), lambda h, i: (kb0, h)),
            pl.BlockSpec((kv_rows, hd), lambda h, i: (kb0, vc0 + h)),
        ],
        out_specs=pl.BlockSpec((tq, gw), lambda h, i: (i, h)),
        out_shape=jax.ShapeDtypeStruct((q_rows, aw), BF16),
        compiler_params=_cparams(("parallel", "parallel"), vm),
        name="attention",
    )(q, k, z1)


def _halo_specs(tr, hr, width, col_block, n_rows):
    per = tr // hr
    last = n_rows // hr - 1
    prev = pl.BlockSpec((hr, width), lambda i, *_: (jnp.maximum(i * per - 1, 0), col_block(*_)))
    nxt = pl.BlockSpec((hr, width), lambda i, *_: (jnp.minimum((i + 1) * per, last), col_block(*_)))
    return prev, nxt


def _edge_flags(i, tr, n_lat, n_rows):
    r0 = i * tr
    r1 = r0 + tr
    pm = jnp.where((r0 == 0) | (r0 == n_lat), 0.0, 1.0).astype(F32)
    nm = jnp.where((r1 == n_lat) | (r1 == n_rows), 0.0, 1.0).astype(F32)
    return pm, nm


def _dwconv3(x, prev_row, next_row, w):
    tr = x.shape[0]
    rid = lax.broadcasted_iota(jnp.int32, x.shape, 0)
    xp = jnp.where(rid == 0, prev_row, pltpu.roll(x, 1, 0))
    xn = jnp.where(rid == tr - 1, next_row, pltpu.roll(x, tr - 1, 0))
    return xp * w[0:1] + x * w[1:2] + xn * w[2:3]


def _conv_branch_kernel(h_ref, hp_ref, hn_ref, b_ref, c_ref, cp_ref, cn_ref, w_ref, o_ref, *, n_lat, n_rows):
    tr = h_ref.shape[0]
    pm, nm = _edge_flags(pl.program_id(0), tr, n_lat, n_rows)
    hr = hp_ref.shape[0]
    u = c_ref[...].astype(F32) * h_ref[...].astype(F32)
    up = cp_ref[hr - 1:hr, :].astype(F32) * hp_ref[hr - 1:hr, :].astype(F32) * pm
    un = cn_ref[0:1, :].astype(F32) * hn_ref[0:1, :].astype(F32) * nm
    o_ref[...] = (b_ref[...].astype(F32) * _dwconv3(u, up, un, w_ref[...])).astype(o_ref.dtype)


def _conv_branch(z1, conv_w, *, rows, col0, cw, n_lat):
    tr, hr = ROW_TILE, SUBLANES_BF16
    assert col0 % cw == 0
    cb = col0 // cw
    main = lambda o: pl.BlockSpec((tr, cw), lambda i: (i, cb + o))
    hp, hn = _halo_specs(tr, hr, cw, lambda: cb, rows)
    cp, cn = _halo_specs(tr, hr, cw, lambda: cb + 2, rows)
    vm = 2 * 4 * tr * cw * 2 + 8 * tr * cw * 4
    return pl.pallas_call(
        functools.partial(_conv_branch_kernel, n_lat=n_lat, n_rows=rows),
        grid=(rows // tr,),
        in_specs=[main(0), hp, hn, main(1), main(2), cp, cn, pl.BlockSpec((3, cw), lambda i: (0, 0))],
        out_specs=pl.BlockSpec((tr, cw), lambda i: (i, 0)),
        out_shape=jax.ShapeDtypeStruct((rows, cw), BF16),
        compiler_params=_cparams(("parallel",), vm),
        name="conv_branch",
    )(z1, z1, z1, z1, z1, z1, z1, conv_w)


def _ffn_act_kernel(v_ref, vp_ref, vn_ref, g_ref, gp_ref, gn_ref, wv_ref, wg_ref, bv_ref, bg_ref, o_ref, *,
                    n_lat, n_rows):
    tr = v_ref.shape[0]
    pm, nm = _edge_flags(pl.program_id(0), tr, n_lat, n_rows)
    hr = vp_ref.shape[0]
    ri = lax.broadcasted_iota(jnp.int32, (2 * tr, tr), 0)
    ci = lax.broadcasted_iota(jnp.int32, (2 * tr, tr), 1)
    shift = jnp.where(ci == jnp.where(ri < tr, ri - 1, ri - tr + 1), 1.0, 0.0).astype(v_ref.dtype)
    r8 = lax.broadcasted_iota(jnp.int32, (SUBLANES_F32, v_ref.shape[1]), 0)

    def conv(x_ref, p_ref, n_ref, w_ref, b_ref):
        x = x_ref[...]
        sh = jnp.dot(shift, x, preferred_element_type=F32)
        first = sh[:SUBLANES_F32] + jnp.where(r8 == 0, p_ref[hr - 1:hr, :].astype(F32) * pm, 0.0)
        lastr = sh[2 * tr - SUBLANES_F32:] + jnp.where(r8 == SUBLANES_F32 - 1, n_ref[0:1, :].astype(F32) * nm, 0.0)
        xp = jnp.concatenate([first, sh[SUBLANES_F32:tr]], axis=0)
        xn = jnp.concatenate([sh[tr:2 * tr - SUBLANES_F32], lastr], axis=0)
        w = w_ref[...]
        return xp * w[0:1] + x.astype(F32) * w[1:2] + xn * w[2:3] + b_ref[...]

    val = conv(v_ref, vp_ref, vn_ref, wv_ref, bv_ref)
    gate = conv(g_ref, gp_ref, gn_ref, wg_ref, bg_ref)
    o_ref[...] = (_silu(gate) * val).astype(o_ref.dtype)


def _ffn_act(u, conv_w, conv_b, *, rows, n_lat):
    ff = u.shape[1] // 2
    tr, hr = ROW_TILE // 2, SUBLANES_BF16
    tc = max(t for t in range(LANES, 6144 + 1, LANES) if ff % t == 0)
    nb = ff // tc
    vp, vn = _halo_specs(tr, hr, tc, lambda j: j, rows)
    gp, gn = _halo_specs(tr, hr, tc, lambda j: nb + j, rows)
    wspec = lambda o: pl.BlockSpec((3, tc), lambda i, j: (0, o * nb + j))
    bspec = lambda o: pl.BlockSpec((1, tc), lambda i, j: (0, o * nb + j))
    vm = 2 * 3 * tr * tc * 2 + 12 * tr * tc * 4
    return pl.pallas_call(
        functools.partial(_ffn_act_kernel, n_lat=n_lat, n_rows=rows),
        grid=(rows // tr, nb),
        in_specs=[
            pl.BlockSpec((tr, tc), lambda i, j: (i, j)), vp, vn,
            pl.BlockSpec((tr, tc), lambda i, j: (i, nb + j)), gp, gn,
            wspec(0), wspec(1), bspec(0), bspec(1),
        ],
        out_specs=pl.BlockSpec((tr, tc), lambda i, j: (i, j)),
        out_shape=jax.ShapeDtypeStruct((rows, ff), BF16),
        compiler_params=_cparams(("parallel", "parallel"), vm),
        name="ffn_act",
    )(u, u, u, u, u, u, conv_w, conv_w, conv_b, conv_b)


def _head_sums(x, hs):
    assert hs & (hs - 1) == 0
    li = lax.broadcasted_iota(jnp.int32, (LANES, LANES), 0) & -hs
    lj = lax.broadcasted_iota(jnp.int32, (LANES, LANES), 1) & -hs
    seg = jnp.where(li == lj, 1.0, 0.0).astype(F32)
    parts = [jnp.dot(x[:, t * LANES:(t + 1) * LANES], seg, preferred_element_type=F32,
                     precision=lax.Precision.HIGHEST) for t in range(x.shape[1] // LANES)]
    return jnp.concatenate(parts, axis=1)


def _rwkv_feat_kernel(z_ref, zp_ref, zn_ref, lo_ref, cw_ref, w2_ref, w0_ref, a2_ref, a0_ref, g2_ref, kk_ref,
                      ka_ref, r_o, v_o, kn_o, ld0_o, ld1_o, kd0_o, kd1_o, b0_o, b1_o, g_o, *,
                      n_lat, n_rows, dl2, il2, hs):
    tr, w3 = z_ref.shape
    W = w3 // 3
    pm, nm = _edge_flags(pl.program_id(0), tr, n_lat, n_rows)
    hr = zp_ref.shape[0]
    rkv = _dwconv3(z_ref[...], zp_ref[hr - 1:hr, :] * pm, zn_ref[0:1, :] * nm, cw_ref[...])
    r, k, v = rkv[:, :W], rkv[:, W:2 * W], rkv[:, 2 * W:]
    lw = lo_ref[:, :dl2]
    la = lo_ref[:, dl2:dl2 + il2]
    lg = lo_ref[:, dl2 + il2:]
    wl = jnp.dot(jnp.tanh(lw).astype(BF16), w2_ref[...], preferred_element_type=F32) + w0_ref[...]
    ld = -jnp.exp(-_softplus(-wl) - 0.5)
    a = _sigmoid(jnp.dot(la.astype(BF16), a2_ref[...], preferred_element_type=F32) + a0_ref[...])
    g = jnp.dot(_sigmoid(lg).astype(BF16), g2_ref[...], preferred_element_type=F32)
    kk = k * kk_ref[...]
    kn = kk / jnp.maximum(jnp.sqrt(_head_sums(kk * kk, hs)), 1e-12)
    ka = ka_ref[...]
    r_o[...] = r
    v_o[...] = v
    kn_o[...] = kn
    g_o[...] = g
    for d, (ld_o, kd_o, b_o) in enumerate(((ld0_o, kd0_o, b0_o), (ld1_o, kd1_o, b1_o))):
        ad = a[:, d * W:(d + 1) * W]
        ld_o[...] = ld[:, d * W:(d + 1) * W]
        kd_o[...] = k * (1.0 + (ad - 1.0) * ka)
        b_o[...] = kn * ad


def _rwkv_features(z2, lo, p, *, n_lat, hs):
    R, w3 = z2.shape
    W = w3 // 3
    lwp = lo.shape[1]
    tr, hr = ROW_TILE // 2, SUBLANES_F32
    dl2, il2 = p["w2"].shape[0], p["a2"].shape[0]
    zp, zn = _halo_specs(tr, hr, w3, lambda: 0, R)
    full = lambda a: pl.BlockSpec(a.shape, lambda i: (0,) * a.ndim)
    out_spec = pl.BlockSpec((tr, W), lambda i: (i, 0))
    vm = 2 * (tr * w3 * 4 + tr * lwp * 4 + 10 * tr * W * 4) + 12 * tr * w3 * 4
    outs = pl.pallas_call(
        functools.partial(_rwkv_feat_kernel, n_lat=n_lat, n_rows=R, dl2=dl2, il2=il2, hs=hs),
        grid=(R // tr,),
        in_specs=[pl.BlockSpec((tr, w3), lambda i: (i, 0)), zp, zn, pl.BlockSpec((tr, lwp), lambda i: (i, 0)),
                  full(p["cw"]), full(p["w2"]), full(p["w0"]), full(p["a2"]), full(p["a0"]), full(p["g2"]),
                  full(p["k_k"]), full(p["k_a"])],
        out_specs=[out_spec] * 10,
        out_shape=[jax.ShapeDtypeStruct((R, W), F32)] * 10,
        compiler_params=_cparams(("parallel",), vm),
        name="rwkv_features",
    )(z2, z2, z2, lo, p["cw"], p["w2"], p["w0"], p["a2"], p["a0"], p["g2"], p["k_k"], p["k_a"])
    return dict(zip(("r", "v", "kn", "ld0", "ld1", "kd0", "kd1", "b0", "b1", "g"), outs))


def _scan_kernel(rf, vf, kf, ldf, kdf, bf, rb, vb, kb, ldb, kdb, bb, yf_o, yb_o, st_ref):
    @pl.when(pl.program_id(0) == 0)
    def _():
        st_ref[...] = jnp.zeros_like(st_ref)

    C, W = rf.shape
    C2 = 2 * C
    dot = functools.partial(jnp.dot, preferred_element_type=F32)
    dot_nt = lambda a, b: lax.dot_general(a, b, (((1,), (1,)), ((), ())), preferred_element_type=F32)
    ri = lax.broadcasted_iota(jnp.int32, (C, C), 0)
    ci = lax.broadcasted_iota(jnp.int32, (C, C), 1)
    ii = lax.broadcasted_iota(jnp.int32, (C2, C2), 0)
    jj = lax.broadcasted_iota(jnp.int32, (C2, C2), 1)
    im, jm = ii & (C - 1), jj & (C - 1)
    eye = jnp.where(ii == jj, 1.0, 0.0).astype(F32)
    same_head = (ii & -C) == (jj & -C)
    h0 = lax.broadcasted_iota(jnp.int32, (C, LANES), 1) < LANES // 2

    def stack(x):
        return jnp.concatenate([jnp.where(h0, x, 0.0), jnp.where(h0, 0.0, x)], axis=0)

    def fold(x):
        return x[:C] + x[C:]

    inst = []
    for d, (r_, v_, k_, ld_, kd_, b_, y_o) in enumerate(((rf, vf, kf, ldf, kdf, bf, yf_o),
                                                          (rb, vb, kb, ldb, kdb, bb, yb_o))):
        rev = d == 1
        ld, kd, b = ld_[...], kd_[...], b_[...]
        inc = jnp.where((ci >= ri) if rev else (ci <= ri), 1.0, 0.0).astype(F32)
        cum = jnp.dot(inc, ld, preferred_element_type=F32, precision=lax.Precision.HIGHEST)
        tot = cum[0:1, :] if rev else cum[C - 1:C, :]
        e_neg = jnp.exp(-cum)
        e_dec = jnp.exp(tot - cum)
        full = dict(kt=k_[...] * jnp.exp(cum - ld), rt=r_[...] * jnp.exp(cum), kh=kd * e_neg, bh=b * e_neg,
                    kdec=kd * e_dec, bdec=b * e_dec, v=v_[...], etot=jnp.exp(tot))
        strict = (jm > im) if rev else (jm < im)
        incl = (jm >= im) if rev else (jm <= im)
        for p in range(W // LANES):
            q = {n: a[:, p * LANES:(p + 1) * LANES] for n, a in full.items()}
            q.update(d=d, p=p, rev=rev, strict=strict, incl=incl, y_o=y_o)
            inst.append(q)

    for q in inst:
        lhs = jnp.concatenate([stack(q["kt"]), stack(q["rt"])], axis=0).astype(BF16)
        rhs = jnp.concatenate([stack(q["bh"]), stack(q["kh"])], axis=0).astype(BF16)
        q["G"] = dot_nt(lhs, rhs)
    for q in inst:
        G = q.pop("G")
        q["Lb"] = jnp.where(q["strict"], G[:C2, :C2], 0.0)
        q["LkAk"] = jnp.concatenate([jnp.where(q["strict"], G[:C2, C2:], 0.0),
                                     jnp.where(q["incl"], G[C2:, C2:], 0.0)], axis=0).astype(BF16)
        q["Ab"] = jnp.where(q["incl"], G[C2:, :C2], 0.0).astype(BF16)

    s = 1
    while s < C:
        same = (ii & -(2 * s)) == (jj & -(2 * s))
        hi_i, hi_j = (ii & s) != 0, (jj & s) != 0
        off = {False: same & hi_i & ~hi_j, True: same & ~hi_i & hi_j}
        if s == 1:
            for q in inst:
                q["T"] = eye - jnp.where(off[q["rev"]], q["Lb"], 0.0)
        else:
            for q in inst:
                q["Tb"] = q["T"].astype(BF16)
                q["TM"] = dot(q["Tb"], jnp.where(off[q["rev"]], q["Lb"], 0.0).astype(BF16))
            for q in inst:
                q["T"] = q["T"] - dot(q.pop("TM").astype(BF16), q.pop("Tb"))
        s *= 2

    for q in inst:
        S = st_ref[q["d"], q["p"]]
        q["S"] = S
        q["X"] = dot_nt(jnp.concatenate([q["kt"], q["rt"]], axis=0).astype(BF16), S.astype(BF16))
        q["LA"] = dot(q.pop("LkAk"), stack(q["v"]).astype(BF16))
    for q in inst:
        q["Us"] = dot(q.pop("T").astype(BF16), (stack(q["X"][:C]) + q["LA"][:C2]).astype(BF16))
    for q in inst:
        q["ABU"] = dot(q.pop("Ab"), q["Us"].astype(BF16))
    for q in inst:
        sl = slice(q["p"] * LANES, (q["p"] + 1) * LANES)
        q["y_o"][:, sl] = q["X"][C:] + fold(q["LA"][C2:] - q.pop("ABU"))
        vu_t = jnp.concatenate([q["v"], -fold(q["Us"])], axis=0).T.astype(BF16)
        upd = dot(vu_t, jnp.concatenate([q["kdec"], q["bdec"]], axis=0).astype(BF16))
        st_ref[q["d"], q["p"]] = q["S"] * q["etot"] + jnp.where(same_head, upd, 0.0)


def _rwkv_scan(f, *, n_lat):
    R, W = f["r"].shape
    C = CHUNK
    n, nl = R // C, n_lat // C
    nc = n - nl
    fwd = lambda s: (jnp.where(s < nc, nl + s, s - nc), 0)
    bwd = lambda s: (n - 1 - s, 0)
    fs, bs = pl.BlockSpec((C, W), fwd), pl.BlockSpec((C, W), bwd)
    vm = 2 * 14 * C * W * 4 + 2 * (W // LANES) * LANES * LANES * 4 + (24 << 20)
    return pl.pallas_call(
        _scan_kernel,
        grid=(n,),
        in_specs=[fs] * 6 + [bs] * 6,
        out_specs=[fs, bs],
        out_shape=[jax.ShapeDtypeStruct((R, W), F32)] * 2,
        scratch_shapes=[pltpu.VMEM((2, W // LANES, LANES, LANES), F32)],
        compiler_params=_cparams(("arbitrary",), vm),
        name="rwkv_scan",
    )(f["r"], f["v"], f["kn"], f["ld0"], f["kd0"], f["b0"], f["r"], f["v"], f["kn"], f["ld1"], f["kd1"], f["b1"])


def _rwkv_readout_kernel(yf_ref, yb_ref, r_ref, v_ref, kd0_ref, kd1_ref, g_ref, rk_ref, lw_ref, lb_ref, o_ref, *,
                         hs):
    y = yf_ref[...] + yb_ref[...]
    mu = _head_sums(y, hs) * (1.0 / hs)
    d = y - mu
    var = _head_sums(d * d, hs) * (1.0 / hs)
    yn = d * lax.rsqrt(var + GN_EPS) * lw_ref[...] + lb_ref[...]
    bonus = _head_sums(r_ref[...] * (kd0_ref[...] + kd1_ref[...]) * rk_ref[...], hs) * v_ref[...]
    o_ref[...] = ((yn + bonus) * g_ref[...]).astype(o_ref.dtype)


def _rwkv_readout(yf, yb, f, p, *, rows, hs):
    W = yf.shape[1]
    tr = ROW_TILE
    rs = pl.BlockSpec((tr, W), lambda i: (i, 0))
    ps = pl.BlockSpec((1, W), lambda i: (0, 0))
    vm = 2 * 8 * tr * W * 4 + 10 * tr * W * 4
    return pl.pallas_call(
        functools.partial(_rwkv_readout_kernel, hs=hs),
        grid=(rows // tr,),
        in_specs=[rs] * 7 + [ps] * 3,
        out_specs=rs,
        out_shape=jax.ShapeDtypeStruct((rows, W), BF16),
        compiler_params=_cparams(("parallel",), vm),
        name="rwkv_readout",
    )(yf, yb, f["r"], f["v"], f["kd0"], f["kd1"], f["g"], p["r_k"], p["ln_w"], p["ln_b"])


def _merge_kernel(a_ref, c_ref, r_ref, wa_ref, wc_ref, wr_ref, ga_ref, gc_ref, gr_ref, o_ref):
    ya = jnp.dot(a_ref[...], wa_ref[...], preferred_element_type=F32)
    yc = jnp.dot(c_ref[...], wc_ref[...], preferred_element_type=F32)
    yr = jnp.dot(r_ref[...], wr_ref[...], preferred_element_type=F32)
    o_ref[...] = (ga_ref[...].astype(F32) * ya + gc_ref[...].astype(F32) * yc
                  + gr_ref[...].astype(F32) * yr).astype(o_ref.dtype)


def _merge(att, cv, rw, wa, wc, wr, layer, gates, *, rows):
    D = wa.shape[2]
    tm = _pick(rows, (512, 256, 128, 64))
    tn = _pick(D, (1024, 512, 256, 128))
    nb = D // tn
    ka, kc, kr = wa.shape[1], wc.shape[1], wr.shape[1]
    vm = 2 * ((tm + tn) * (ka + kc + kr) * 2 + 4 * tm * tn * 2) + 4 * tm * tn * 4
    return pl.pallas_call(
        _merge_kernel,
        grid=(nb, rows // tm),
        in_specs=[
            pl.BlockSpec((tm, ka), lambda j, i: (i, 0)),
            pl.BlockSpec((tm, kc), lambda j, i: (i, 0)),
            pl.BlockSpec((tm, kr), lambda j, i: (i, 0)),
            pl.BlockSpec((None, ka, tn), lambda j, i: (layer, 0, j)),
            pl.BlockSpec((None, kc, tn), lambda j, i: (layer, 0, j)),
            pl.BlockSpec((None, kr, tn), lambda j, i: (layer, 0, j)),
            pl.BlockSpec((tm, tn), lambda j, i: (i, j)),
            pl.BlockSpec((tm, tn), lambda j, i: (i, nb + j)),
            pl.BlockSpec((tm, tn), lambda j, i: (i, 2 * nb + j)),
        ],
        out_specs=pl.BlockSpec((tm, tn), lambda j, i: (i, j)),
        out_shape=jax.ShapeDtypeStruct((rows, D), BF16),
        compiler_params=_cparams(("parallel", "parallel"), vm),
        name="merge",
    )(att, cv, rw, wa, wc, wr, gates, gates, gates)


def _rope_tables(n_lat, n_ctx, hd):
    rows = n_lat // GRID_W
    row = jnp.repeat(jnp.arange(rows, dtype=F32), GRID_W)
    col = jnp.tile(jnp.arange(GRID_W, dtype=F32), rows)
    half = hd // 2
    inv_freq = ROPE_THETA ** (-jnp.arange(0, half, 2, dtype=F32) / half)
    ar, ac = row[:, None] * inv_freq, col[:, None] * inv_freq
    z = jnp.zeros_like(ar)
    cos = jnp.concatenate([jnp.cos(ar), jnp.cos(ar), jnp.cos(ac), jnp.cos(ac)], axis=-1)
    sa = jnp.concatenate([-jnp.sin(ar), z, -jnp.sin(ac), z], axis=-1)
    sb = jnp.concatenate([z, jnp.sin(ar), z, jnp.sin(ac)], axis=-1)
    pad = lambda t, v: jnp.concatenate([t, jnp.full((n_ctx, hd), v, F32)], axis=0)
    return pad(cos, 1.0), pad(sa, 0.0), pad(sb, 0.0)


def _block_diag2(w):
    z = jnp.zeros_like(w[0])
    return jnp.concatenate([jnp.concatenate([w[0], z], axis=1), jnp.concatenate([z, w[1]], axis=1)], axis=0)


def kernel(x, c, ctx, c_ctx, w_mod, b_mod, norm1_g, norm2_g, w_in, b_gate, q_norm_g, k_norm_g, w_attn_o, conv_w, w_conv_o, rwkv_conv_w, rwkv_w0, rwkv_w2, rwkv_a0, rwkv_a2, rwkv_g2, rwkv_k_k, rwkv_k_a, rwkv_r_k, rwkv_ln_w, rwkv_ln_b, w_rwkv_o, w_out, ffn_w_up, ffn_conv_w, ffn_conv_b, ffn_w_down, final_g):
    B, S, D = x.shape
    assert B == 1, "batch is folded away; the problem fixes BATCH = 1"
    Tc = ctx.shape[1]
    R = S + Tc
    L = w_mod.shape[0]
    hd = q_norm_g.shape[1]
    aw = w_attn_o.shape[1]
    cw = conv_w.shape[2]
    W = rwkv_w0.shape[2]
    hs = rwkv_r_k.shape[2]
    dl, il, gl = rwkv_w2.shape[2], rwkv_a2.shape[2], rwkv_g2.shape[1]
    ff = ffn_w_down.shape[1]
    kvw = (w_in.shape[2] - (aw + 3 * cw + 3 * W + 2 * dl + 2 * il + gl + 3 * D)) // 2
    assert S % ROW_TILE == 0 and Tc % ROW_TILE == 0 and S % GRID_W == 0
    assert hd == LANES and 2 * hs == LANES and 2 * CHUNK == LANES and W % LANES == 0

    o_conv = aw + 2 * kvw
    o_rkv = o_conv + 3 * cw
    o_lora = o_rkv + 3 * W
    o_gate = o_lora + 2 * dl + 2 * il + gl
    lw_real = 2 * dl + 2 * il + gl
    lwp = -(-lw_real // LANES) * LANES
    gp = lwp - 2 * dl - 2 * il

    xa = jnp.concatenate([x[0], ctx[0]], axis=0)
    cond = jnp.zeros((16, D), F32).at[0].set(c[0]).at[1].set(c_ctx)
    mods = _ada_mod(cond, w_mod, b_mod)
    tabs = _rope_tables(S, Tc, hd)

    w_i, w_up, w_dn, w_ao, w_co, w_ro, w_o = (
        w.astype(BF16) for w in (w_in, ffn_w_up, ffn_w_down, w_attn_o, w_conv_o, w_rwkv_o, w_out))
    g_off = o_gate - o_lora
    g_wide = -(-(g_off + 3 * D) // LANES) * LANES
    assert o_lora % LANES == 0 and o_lora + g_wide <= -(-w_in.shape[2] // LANES) * LANES
    b_gate_p = jnp.pad(b_gate, ((0, 0), (g_off, g_wide - g_off - 3 * D)))

    for l in range(L):
        last = l == L - 1
        rows = S if last else R
        m = mods[l, :2].reshape(2, N_MOD, 1, D)
        mod = lambda i: m[:, i]
        rp = {
            "cw": rwkv_conv_w[l],
            "w2": _block_diag2(rwkv_w2[l]).astype(BF16),
            "w0": rwkv_w0[l].reshape(1, 2 * W),
            "a2": _block_diag2(rwkv_a2[l]).astype(BF16),
            "a0": rwkv_a0[l].reshape(1, 2 * W),
            "g2": jnp.pad(rwkv_g2[l], ((0, gp - gl), (0, 0))).astype(BF16),
            "k_k": rwkv_k_k[l].reshape(1, W),
            "k_a": rwkv_k_a[l].reshape(1, W),
            "r_k": rwkv_r_k[l].reshape(1, W),
            "ln_w": rwkv_ln_w[l].reshape(1, W),
            "ln_b": rwkv_ln_b[l].reshape(1, W),
        }

        h = _norm_mod(xa, norm1_g[l], mod(0), mod(1), S, R)
        z1 = _matmul(h, w_i, l, rows=R, out_dtype=BF16, col0=0, n=o_rkv, name="in_proj_qkv_conv")
        z2 = _matmul(h, w_i, l, rows=R, out_dtype=F32, col0=o_rkv, n=3 * W, name="in_proj_rkv")
        lo = _matmul(h, w_i, l, rows=R, out_dtype=F32, col0=o_lora, n=lwp, name="in_proj_lora")
        gates = _matmul(h, w_i, l, rows=rows, out_dtype=BF16, col0=o_lora, n=g_wide, epi="sigmoid_bias",
                        bias=b_gate_p[l], name="in_proj_gates")[:, g_off:g_off + 3 * D]

        qn, kn = _qk_prep(z1, tabs, q_norm_g[l], k_norm_g[l], aw, kvw, hd)
        att = _attention(qn, kn, z1, q_row0=0, q_rows=S, kv_row0=0, kv_rows=R, v_col0=aw + kvw, hd=hd)
        if not last:
            att_c = _attention(qn, kn, z1, q_row0=S, q_rows=Tc, kv_row0=S, kv_rows=Tc, v_col0=aw + kvw, hd=hd)
            att = jnp.concatenate([att, att_c], axis=0)

        cv = _conv_branch(z1, conv_w[l], rows=rows, col0=o_conv, cw=cw, n_lat=S)

        feat = _rwkv_features(z2, lo, rp, n_lat=S, hs=hs)
        yf, yb = _rwkv_scan(feat, n_lat=S)
        rw = _rwkv_readout(yf, yb, feat, rp, rows=rows, hs=hs)

        mg = _merge(att, cv, rw, w_ao, w_co, w_ro, l, gates, rows=rows)
        xa = _matmul(mg, w_o, l, rows=rows, out_dtype=F32, epi="residual", res=xa, gate2=mod(2).reshape(2, D),
                     n_lat=S, name="out_proj")

        h2 = _norm_mod(xa, norm2_g[l], mod(3), mod(4), S, rows)
        u = _matmul(h2, w_up, l, rows=rows, out_dtype=BF16, name="ffn_up")
        act = _ffn_act(u, ffn_conv_w[l], ffn_conv_b[l].reshape(1, 2 * ff), rows=rows, n_lat=S)
        xa = _matmul(act, w_dn, l, rows=rows, out_dtype=F32, epi="residual", res=xa, gate2=mod(5).reshape(2, D),
                     n_lat=S, name="ffn_down")

    return _final_norm(xa, final_g, S)[None]
```

```python
import functools

import jax
import jax.numpy as jnp
from jax import lax
from jax.experimental import pallas as pl
from jax.experimental.pallas import tpu as pltpu

F32 = jnp.float32
BF16 = jnp.bfloat16

GRID_W = 64
ROPE_THETA = 10000.0
Q_PER_KV = 4
N_MOD = 6
NORM_EPS = 1e-6
GN_EPS = 64e-5
LOG2_E = 1.4426950408889634

LANES = 128
SUBLANES_F32 = 8
SUBLANES_BF16 = 16
VMEM_LIMIT_CAP = 56 * 1024 * 1024
MM_VMEM_BUDGET = 48 * 1024 * 1024
CHUNK = 64
ROW_TILE = 256


def _pick(n, candidates):
    for c in candidates:
        if n % c == 0:
            return c
    raise ValueError(f"no tile in {candidates} divides {n}")


def _cparams(sem, vmem_bytes):
    limit = int(min(max(vmem_bytes * 5 // 4 + (4 << 20), 16 << 20), VMEM_LIMIT_CAP))
    return pltpu.CompilerParams(dimension_semantics=sem, vmem_limit_bytes=limit)


def _sigmoid(x):
    return 1.0 / (1.0 + jnp.exp(-x))


def _silu(x):
    return x * _sigmoid(x)


def _softplus(x):
    return jnp.maximum(x, 0.0) + jnp.log(1.0 + jnp.exp(-jnp.abs(x)))


def _mod_kernel(c_ref, w_ref, b_ref, o_ref):
    a = _silu(c_ref[...]).astype(BF16)
    o_ref[...] = jnp.dot(a, w_ref[...].astype(BF16), preferred_element_type=F32) + b_ref[...]


def _ada_mod(cond, w_mod, b_mod):
    L, D, N = w_mod.shape
    tn = _pick(N, (512, 256, 128))
    vm = 2 * (D * tn * 4 + 16 * tn * 4 * 2) + 16 * D * 4 * 2 + D * tn * 2
    return pl.pallas_call(
        _mod_kernel,
        grid=(L, N // tn),
        in_specs=[
            pl.BlockSpec((16, D), lambda l, j: (0, 0)),
            pl.BlockSpec((None, D, tn), lambda l, j: (l, 0, j)),
            pl.BlockSpec((None, 1, tn), lambda l, j: (l, 0, j)),
        ],
        out_specs=pl.BlockSpec((None, 16, tn), lambda l, j: (l, 0, j)),
        out_shape=jax.ShapeDtypeStruct((L, 16, N), F32),
        compiler_params=_cparams(("parallel", "parallel"), vm),
        name="ada_mod",
    )(cond, w_mod, b_mod.reshape(L, 1, N))


def _norm_mod_kernel(x_ref, g_ref, sh_ref, sc_ref, o_ref):
    x = x_ref[...]
    y = x * lax.rsqrt(jnp.mean(x * x, axis=-1, keepdims=True) + NORM_EPS) * g_ref[...]
    o_ref[...] = (y * (1.0 + sc_ref[...]) + sh_ref[...]).astype(o_ref.dtype)


def _norm_mod(x, g, shift2, scale2, n_lat, rows):
    D = x.shape[1]
    tr = ROW_TILE
    nl = n_lat // tr
    vm = 2 * (tr * D * 4 + tr * D * 2) + 6 * D * 4
    return pl.pallas_call(
        _norm_mod_kernel,
        grid=(rows // tr,),
        in_specs=[
            pl.BlockSpec((tr, D), lambda i: (i, 0)),
            pl.BlockSpec((1, D), lambda i: (0, 0)),
            pl.BlockSpec((None, 1, D), lambda i: (jnp.where(i >= nl, 1, 0), 0, 0)),
            pl.BlockSpec((None, 1, D), lambda i: (jnp.where(i >= nl, 1, 0), 0, 0)),
        ],
        out_specs=pl.BlockSpec((tr, D), lambda i: (i, 0)),
        out_shape=jax.ShapeDtypeStruct((rows, D), BF16),
        compiler_params=_cparams(("parallel",), vm),
        name="norm_mod",
    )(x, g.reshape(1, D), shift2, scale2)


def _final_norm_kernel(x_ref, g_ref, o_ref):
    x = x_ref[...]
    o_ref[...] = x * lax.rsqrt(jnp.mean(x * x, axis=-1, keepdims=True) + NORM_EPS) * g_ref[...]


def _final_norm(x, g, rows):
    D = x.shape[1]
    tr = ROW_TILE
    return pl.pallas_call(
        _final_norm_kernel,
        grid=(rows // tr,),
        in_specs=[pl.BlockSpec((tr, D), lambda i: (i, 0)), pl.BlockSpec((1, D), lambda i: (0, 0))],
        out_specs=pl.BlockSpec((tr, D), lambda i: (i, 0)),
        out_shape=jax.ShapeDtypeStruct((rows, D), F32),
        compiler_params=_cparams(("parallel",), 4 * tr * D * 4),
        name="final_norm",
    )(x, g.reshape(1, D))


def _mm_kernel(*refs, nk, epi, tm, n_lat):
    if nk > 1:
        acc_ref = refs[-1]
        refs = refs[:-1]
    a_ref, b_ref = refs[0], refs[1]
    o_ref = refs[-1]
    extra = refs[2:-1]
    row0 = pl.program_id(0) * tm

    def finish(acc):
        if epi == "cast":
            o_ref[...] = acc.astype(o_ref.dtype)
        elif epi == "sigmoid_bias":
            o_ref[...] = _sigmoid(acc + extra[0][...]).astype(o_ref.dtype)
        elif epi == "residual":
            res_ref, gate_ref = extra
            row = row0 + lax.broadcasted_iota(jnp.int32, acc.shape, 0)
            gate = jnp.where(row < n_lat, gate_ref[0:1, :], gate_ref[1:2, :])
            o_ref[...] = res_ref[...] + gate * acc
        else:
            raise ValueError(epi)

    part = jnp.dot(a_ref[...], b_ref[...], preferred_element_type=F32)
    if nk == 1:
        finish(part)
        return
    k = pl.program_id(2)

    @pl.when(k == 0)
    def _():
        acc_ref[...] = part

    @pl.when(k > 0)
    def _():
        acc_ref[...] += part

    @pl.when(k == nk - 1)
    def _():
        finish(acc_ref[...])


def _mm_tiles(rows, n, col0, K, osz, residual):
    best = None
    for tm in (1408, 1024, 768, 512, 256, 128, 64):
        if rows % tm:
            continue
        for tn in (2048, 1024, 512, 256, 128, n):
            if n % tn or col0 % tn:
                continue
            for tk in ((K,) if K <= 4096 else range(LANES, 5632 + 1, LANES)):
                if K % tk:
                    continue
                vm = 2 * (tm * tk * 2 + tk * tn * 2 + tm * tn * osz) + tm * tn * 4
                vm += tm * tn * 4 if tk < K else 0
                vm += 2 * tm * tn * 4 if residual else 0
                if vm > MM_VMEM_BUDGET:
                    continue
                key = (tm * tn / (tm + tn), tk)
                if best is None or key > best[0]:
                    best = (key, (tm, tn, tk), vm)
    assert best is not None
    return best[1], best[2]


def _matmul(a, b, layer, *, rows, out_dtype, col0=0, n=None, epi="cast", bias=None, res=None, gate2=None,
            n_lat=0, name="mm"):
    K = b.shape[1]
    n = b.shape[2] - col0 if n is None else n
    (tm, tn, tk), vm = _mm_tiles(rows, n, col0, K, jnp.dtype(out_dtype).itemsize, epi == "residual")
    nk = K // tk
    cb = col0 // tn
    in_specs = [
        pl.BlockSpec((tm, tk), lambda i, j, k: (i, k)),
        pl.BlockSpec((None, tk, tn), lambda i, j, k: (layer, k, cb + j)),
    ]
    args = [a, b]
    if epi == "sigmoid_bias":
        in_specs.append(pl.BlockSpec((1, tn), lambda i, j, k: (0, j)))
        args.append(bias.reshape(1, n))
    elif epi == "residual":
        in_specs.append(pl.BlockSpec((tm, tn), lambda i, j, k: (i, j)))
        in_specs.append(pl.BlockSpec((2, tn), lambda i, j, k: (0, j)))
        args += [res, gate2]
    scratch = [pltpu.VMEM((tm, tn), F32)] if nk > 1 else []
    return pl.pallas_call(
        functools.partial(_mm_kernel, nk=nk, epi=epi, tm=tm, n_lat=n_lat),
        grid=(rows // tm, n // tn, nk),
        in_specs=in_specs,
        out_specs=pl.BlockSpec((tm, tn), lambda i, j, k: (i, j)),
        out_shape=jax.ShapeDtypeStruct((rows, n), out_dtype),
        scratch_shapes=scratch,
        compiler_params=_cparams(("parallel", "parallel", "arbitrary"), vm),
        name=name,
    )(*args)


def _mm_wres_kernel(*refs, kc, transposed, sigmoid_bias):
    a_ref, w_ref = refs[0], refs[1]
    o_ref, wb_ref = refs[-2], refs[-1]

    @pl.when(pl.program_id(1) == 0)
    def _():
        for k0 in range(0, w_ref.shape[0], kc):
            wb_ref[k0:k0 + kc, :] = w_ref[k0:k0 + kc, :].astype(BF16)

    if transposed:
        acc = lax.dot_general(a_ref[...], wb_ref[...], (((1,), (1,)), ((), ())), preferred_element_type=F32)
    else:
        acc = jnp.dot(a_ref[...], wb_ref[...], preferred_element_type=F32)
    if sigmoid_bias:
        acc = _sigmoid(acc + refs[2][...])
    o_ref[...] = acc.astype(o_ref.dtype)


def _matmul_wres(a, w, layer, *, rows, out_dtype, col0=0, n=None, transposed=False, bias=None, name="mm_wres"):
    K = w.shape[2] if transposed else w.shape[1]
    n = (w.shape[1] if transposed else w.shape[2]) - col0 if n is None else n
    tn = _pick(n, tuple(t for t in (512, 256, 128) if col0 % t == 0))
    osz = jnp.dtype(out_dtype).itemsize
    for tm in (1408, 1024, 768, 512, 256, 128, 64):
        vm = 2 * (K * tn * 4 + tm * K * 2 + tm * tn * osz) + K * tn * 2 + tm * tn * 4
        if rows % tm == 0 and vm <= MM_VMEM_BUDGET:
            break
    cb = col0 // tn
    if transposed:
        w_spec = pl.BlockSpec((None, tn, K), lambda j, i: (layer, cb + j, 0))
        wb, kc = pltpu.VMEM((tn, K), BF16), _pick(tn, (128,))
    else:
        w_spec = pl.BlockSpec((None, K, tn), lambda j, i: (layer, 0, cb + j))
        wb, kc = pltpu.VMEM((K, tn), BF16), _pick(K, (512, 256, 128))
    in_specs = [pl.BlockSpec((tm, K), lambda j, i: (i, 0)), w_spec]
    args = [a, w]
    if bias is not None:
        in_specs.append(pl.BlockSpec((1, tn), lambda j, i: (0, j)))
        args.append(bias.reshape(1, n))
    return pl.pallas_call(
        functools.partial(_mm_wres_kernel, kc=kc, transposed=transposed, sigmoid_bias=bias is not None),
        grid=(n // tn, rows // tm),
        in_specs=in_specs,
        out_specs=pl.BlockSpec((tm, tn), lambda j, i: (i, j)),
        out_shape=jax.ShapeDtypeStruct((rows, n), out_dtype),
        scratch_shapes=[wb],
        compiler_params=_cparams(("arbitrary", "arbitrary"), vm),
        name=name,
    )(*args)


def _qk_prep_kernel(zq_ref, zk_ref, c_ref, sa_ref, sb_ref, gq_ref, gk_ref, q_ref, k_ref, *, hd):
    cos, sa, sb = c_ref[...], sa_ref[...], sb_ref[...]

    def norm_rope(x, g):
        y = x * lax.rsqrt(jnp.mean(x * x, axis=-1, keepdims=True) + NORM_EPS) * g
        return y * cos + pltpu.roll(y, hd - hd // 4, 1) * sa + pltpu.roll(y, hd // 4, 1) * sb

    gq = gq_ref[...] * (hd ** -0.5 * LOG2_E)
    for h in range(zq_ref.shape[1] // hd):
        sl = slice(h * hd, (h + 1) * hd)
        q_ref[:, sl] = norm_rope(zq_ref[:, sl].astype(F32), gq).astype(BF16)
    gk = gk_ref[...]
    for h in range(zk_ref.shape[1] // hd):
        sl = slice(h * hd, (h + 1) * hd)
        k_ref[:, sl] = norm_rope(zk_ref[:, sl].astype(F32), gk).astype(BF16)


def _qk_prep(z1, tabs, gq, gk, aw, kvw, hd):
    R = z1.shape[0]
    tr = ROW_TILE
    assert aw % kvw == 0
    cos, sa, sb = tabs
    tab_spec = pl.BlockSpec((tr, hd), lambda i: (i, 0))
    vm = 2 * 2 * (tr * aw * 2 + tr * kvw * 2) + 6 * tr * hd * 4 + 8 * tr * hd * 4
    return pl.pallas_call(
        functools.partial(_qk_prep_kernel, hd=hd),
        grid=(R // tr,),
        in_specs=[
            pl.BlockSpec((tr, aw), lambda i: (i, 0)),
            pl.BlockSpec((tr, kvw), lambda i: (i, aw // kvw)),
            tab_spec, tab_spec, tab_spec,
            pl.BlockSpec((1, hd), lambda i: (0, 0)),
            pl.BlockSpec((1, hd), lambda i: (0, 0)),
        ],
        out_specs=[pl.BlockSpec((tr, aw), lambda i: (i, 0)), pl.BlockSpec((tr, kvw), lambda i: (i, 0))],
        out_shape=[jax.ShapeDtypeStruct((R, aw), BF16), jax.ShapeDtypeStruct((R, kvw), BF16)],
        compiler_params=_cparams(("parallel",), vm),
        name="qk_prep",
    )(z1, z1, cos, sa, sb, gq.reshape(1, hd), gk.reshape(1, hd))


def _attn_kernel(q_ref, k_ref, v_ref, o_ref, *, hd, tk, nkv):
    tq = q_ref.shape[0]
    q4 = jnp.concatenate([q_ref[:, g * hd:(g + 1) * hd] for g in range(Q_PER_KV)], axis=0)

    m = l = acc = None
    for j in range(nkv):
        kc = k_ref[j * tk:(j + 1) * tk, :]
        vc = v_ref[j * tk:(j + 1) * tk, :]
        s = lax.dot_general(q4, kc, (((1,), (1,)), ((), ())), preferred_element_type=F32)
        smax = jnp.max(s, axis=-1, keepdims=True)
        if j == 0:
            m = smax
            p = jnp.exp2(s - m)
            l = jnp.sum(p, axis=-1, keepdims=True)
            acc = jnp.dot(p.astype(BF16), vc, preferred_element_type=F32)
        else:
            m_new = jnp.maximum(m, smax)
            alpha = jnp.exp2(m - m_new)
            p = jnp.exp2(s - m_new)
            l = alpha * l + jnp.sum(p, axis=-1, keepdims=True)
            acc = alpha * acc + jnp.dot(p.astype(BF16), vc, preferred_element_type=F32)
            m = m_new
    o = acc / l
    for g in range(Q_PER_KV):
        o_ref[:, g * hd:(g + 1) * hd] = o[g * tq:(g + 1) * tq].astype(o_ref.dtype)


def _attention(q, k, z1, *, q_row0, q_rows, kv_row0, kv_rows, v_col0, hd):
    aw, kvw = q.shape[1], k.shape[1]
    n_kv = kvw // hd
    gw = Q_PER_KV * hd
    tq = _pick(q_rows, (256, 128))
    tk = _pick(kv_rows, (768, 512, 256, 128))
    assert q_row0 % tq == 0 and kv_row0 % kv_rows == 0 and v_col0 % hd == 0
    qb0, kb0, vc0 = q_row0 // tq, kv_row0 // kv_rows, v_col0 // hd
    vm = 2 * (2 * tq * gw * 2 + 2 * kv_rows * hd * 2) + 6 * Q_PER_KV * tq * tk * 4
    return pl.pallas_call(
        functools.partial(_attn_kernel, hd=hd, tk=tk, nkv=kv_rows // tk),
        grid=(n_kv, q_rows // tq),
        in_specs=[
            pl.BlockSpec((tq, gw), lambda h, i: (qb0 + i, h)),
            pl.BlockSpec((kv_rows, hd), lambda h, i: (kb0, h)),
            pl.BlockSpec((kv_rows, hd), lambda h, i: (kb0, vc0 + h)),
        ],
        out_specs=pl.BlockSpec((tq, gw), lambda h, i: (i, h)),
        out_shape=jax.ShapeDtypeStruct((q_rows, aw), BF16),
        compiler_params=_cparams(("parallel", "parallel"), vm),
        name="attention",
    )(q, k, z1)


def _halo_specs(tr, hr, width, col_block, n_rows):
    per = tr // hr
    last = n_rows // hr - 1
    prev = pl.BlockSpec((hr, width), lambda i, *_: (jnp.maximum(i * per - 1, 0), col_block(*_)))
    nxt = pl.BlockSpec((hr, width), lambda i, *_: (jnp.minimum((i + 1) * per, last), col_block(*_)))
    return prev, nxt


def _edge_flags(i, tr, n_lat, n_rows):
    r0 = i * tr
    r1 = r0 + tr
    pm = jnp.where((r0 == 0) | (r0 == n_lat), 0.0, 1.0).astype(F32)
    nm = jnp.where((r1 == n_lat) | (r1 == n_rows), 0.0, 1.0).astype(F32)
    return pm, nm


def _dwconv3(x, prev_row, next_row, w):
    tr = x.shape[0]
    rid = lax.broadcasted_iota(jnp.int32, x.shape, 0)
    xp = jnp.where(rid == 0, prev_row, pltpu.roll(x, 1, 0))
    xn = jnp.where(rid == tr - 1, next_row, pltpu.roll(x, tr - 1, 0))
    return xp * w[0:1] + x * w[1:2] + xn * w[2:3]


def _conv_branch_kernel(h_ref, hp_ref, hn_ref, b_ref, c_ref, cp_ref, cn_ref, w_ref, o_ref, *, n_lat, n_rows):
    tr = h_ref.shape[0]
    pm, nm = _edge_flags(pl.program_id(0), tr, n_lat, n_rows)
    hr = hp_ref.shape[0]
    u = c_ref[...].astype(F32) * h_ref[...].astype(F32)
    up = cp_ref[hr - 1:hr, :].astype(F32) * hp_ref[hr - 1:hr, :].astype(F32) * pm
    un = cn_ref[0:1, :].astype(F32) * hn_ref[0:1, :].astype(F32) * nm
    o_ref[...] = (b_ref[...].astype(F32) * _dwconv3(u, up, un, w_ref[...])).astype(o_ref.dtype)


def _conv_branch(z1, conv_w, *, rows, col0, cw, n_lat):
    tr, hr = ROW_TILE, SUBLANES_BF16
    assert col0 % cw == 0
    cb = col0 // cw
    main = lambda o: pl.BlockSpec((tr, cw), lambda i: (i, cb + o))
    hp, hn = _halo_specs(tr, hr, cw, lambda: cb, rows)
    cp, cn = _halo_specs(tr, hr, cw, lambda: cb + 2, rows)
    vm = 2 * 4 * tr * cw * 2 + 8 * tr * cw * 4
    return pl.pallas_call(
        functools.partial(_conv_branch_kernel, n_lat=n_lat, n_rows=rows),
        grid=(rows // tr,),
        in_specs=[main(0), hp, hn, main(1), main(2), cp, cn, pl.BlockSpec((3, cw), lambda i: (0, 0))],
        out_specs=pl.BlockSpec((tr, cw), lambda i: (i, 0)),
        out_shape=jax.ShapeDtypeStruct((rows, cw), BF16),
        compiler_params=_cparams(("parallel",), vm),
        name="conv_branch",
    )(z1, z1, z1, z1, z1, z1, z1, conv_w)


def _ffn_act_kernel(v_ref, vp_ref, vn_ref, g_ref, gp_ref, gn_ref, wv_ref, wg_ref, bv_ref, bg_ref, o_ref, *,
                    n_lat, n_rows):
    tr = v_ref.shape[0]
    pm, nm = _edge_flags(pl.program_id(0), tr, n_lat, n_rows)
    hr = vp_ref.shape[0]
    ri = lax.broadcasted_iota(jnp.int32, (2 * tr, tr), 0)
    ci = lax.broadcasted_iota(jnp.int32, (2 * tr, tr), 1)
    shift = jnp.where(ci == jnp.where(ri < tr, ri - 1, ri - tr + 1), 1.0, 0.0).astype(v_ref.dtype)
    r8 = lax.broadcasted_iota(jnp.int32, (SUBLANES_F32, v_ref.shape[1]), 0)

    def conv(x_ref, p_ref, n_ref, w_ref, b_ref):
        x = x_ref[...]
        sh = jnp.dot(shift, x, preferred_element_type=F32)
        first = sh[:SUBLANES_F32] + jnp.where(r8 == 0, p_ref[hr - 1:hr, :].astype(F32) * pm, 0.0)
        lastr = sh[2 * tr - SUBLANES_F32:] + jnp.where(r8 == SUBLANES_F32 - 1, n_ref[0:1, :].astype(F32) * nm, 0.0)
        xp = jnp.concatenate([first, sh[SUBLANES_F32:tr]], axis=0)
        xn = jnp.concatenate([sh[tr:2 * tr - SUBLANES_F32], lastr], axis=0)
        w = w_ref[...]
        return xp * w[0:1] + x.astype(F32) * w[1:2] + xn * w[2:3] + b_ref[...]

    val = conv(v_ref, vp_ref, vn_ref, wv_ref, bv_ref)
    gate = conv(g_ref, gp_ref, gn_ref, wg_ref, bg_ref)
    tc = v_ref.shape[1]
    o_ref[:, :tc] = (_silu(gate) * val).astype(o_ref.dtype)
    if o_ref.shape[1] > tc:
        o_ref[:, tc:] = jnp.zeros((tr, o_ref.shape[1] - tc), o_ref.dtype)


def _ffn_blocks(ff):
    tc = max(t for t in range(LANES, 6144 + 1, LANES) if ff % t == 0)
    nb = ff // tc
    pad = next(p for p in range(0, 1024 + 1, LANES) if (nb * (tc + p)) % 1024 == 0)
    return tc, nb, pad


def _ffn_act(u, conv_w, conv_b, *, rows, n_lat):
    ff = u.shape[1] // 2
    tr, hr = ROW_TILE // 2, SUBLANES_BF16
    tc, nb, pad = _ffn_blocks(ff)
    vp, vn = _halo_specs(tr, hr, tc, lambda j: j, rows)
    gp, gn = _halo_specs(tr, hr, tc, lambda j: nb + j, rows)
    wspec = lambda o: pl.BlockSpec((3, tc), lambda i, j: (0, o * nb + j))
    bspec = lambda o: pl.BlockSpec((1, tc), lambda i, j: (0, o * nb + j))
    vm = 2 * 3 * tr * tc * 2 + 12 * tr * tc * 4
    return pl.pallas_call(
        functools.partial(_ffn_act_kernel, n_lat=n_lat, n_rows=rows),
        grid=(rows // tr, nb),
        in_specs=[
            pl.BlockSpec((tr, tc), lambda i, j: (i, j)), vp, vn,
            pl.BlockSpec((tr, tc), lambda i, j: (i, nb + j)), gp, gn,
            wspec(0), wspec(1), bspec(0), bspec(1),
        ],
        out_specs=pl.BlockSpec((tr, tc + pad), lambda i, j: (i, j)),
        out_shape=jax.ShapeDtypeStruct((rows, nb * (tc + pad)), BF16),
        compiler_params=_cparams(("parallel", "parallel"), vm),
        name="ffn_act",
    )(u, u, u, u, u, u, conv_w, conv_w, conv_b, conv_b)


def _head_sums(x, hs):
    assert hs & (hs - 1) == 0
    li = lax.broadcasted_iota(jnp.int32, (LANES, LANES), 0) & -hs
    lj = lax.broadcasted_iota(jnp.int32, (LANES, LANES), 1) & -hs
    seg = jnp.where(li == lj, 1.0, 0.0).astype(F32)
    parts = [jnp.dot(x[:, t * LANES:(t + 1) * LANES], seg, preferred_element_type=F32,
                     precision=lax.Precision.HIGHEST) for t in range(x.shape[1] // LANES)]
    return jnp.concatenate(parts, axis=1)


def _rwkv_feat_kernel(z_ref, zp_ref, zn_ref, lo_ref, cw_ref, w2_ref, w0_ref, a2_ref, a0_ref, g2_ref, kk_ref,
                      ka_ref, r_o, v_o, kn_o, ld0_o, ld1_o, kd0_o, kd1_o, b0_o, b1_o, g_o, *,
                      n_lat, n_rows, dl2, il2, hs):
    tr, w3 = z_ref.shape
    W = w3 // 3
    pm, nm = _edge_flags(pl.program_id(0), tr, n_lat, n_rows)
    hr = zp_ref.shape[0]
    rkv = _dwconv3(z_ref[...], zp_ref[hr - 1:hr, :] * pm, zn_ref[0:1, :] * nm, cw_ref[...])
    r, k, v = rkv[:, :W], rkv[:, W:2 * W], rkv[:, 2 * W:]
    lw = lo_ref[:, :dl2]
    la = lo_ref[:, dl2:dl2 + il2]
    lg = lo_ref[:, dl2 + il2:]
    wl = jnp.dot(jnp.tanh(lw).astype(BF16), w2_ref[...], preferred_element_type=F32) + w0_ref[...]
    ld = -jnp.exp(-_softplus(-wl) - 0.5)
    a = _sigmoid(jnp.dot(la.astype(BF16), a2_ref[...], preferred_element_type=F32) + a0_ref[...])
    g = jnp.dot(_sigmoid(lg).astype(BF16), g2_ref[...], preferred_element_type=F32)
    kk = k * kk_ref[...]
    kn = kk / jnp.maximum(jnp.sqrt(_head_sums(kk * kk, hs)), 1e-12)
    ka = ka_ref[...]
    r_o[...] = r
    v_o[...] = v
    kn_o[...] = kn
    g_o[...] = g
    for d, (ld_o, kd_o, b_o) in enumerate(((ld0_o, kd0_o, b0_o), (ld1_o, kd1_o, b1_o))):
        ad = a[:, d * W:(d + 1) * W]
        ld_o[...] = ld[:, d * W:(d + 1) * W]
        kd_o[...] = k * (1.0 + (ad - 1.0) * ka)
        b_o[...] = kn * ad


def _rwkv_features(z2, lo, p, *, n_lat, hs):
    R, w3 = z2.shape
    W = w3 // 3
    lwp = lo.shape[1]
    tr, hr = ROW_TILE // 2, SUBLANES_F32
    dl2, il2 = p["w2"].shape[0], p["a2"].shape[0]
    zp, zn = _halo_specs(tr, hr, w3, lambda: 0, R)
    full = lambda a: pl.BlockSpec(a.shape, lambda i: (0,) * a.ndim)
    out_spec = pl.BlockSpec((tr, W), lambda i: (i, 0))
    vm = 2 * (tr * w3 * 4 + tr * lwp * 4 + 10 * tr * W * 4) + 12 * tr * w3 * 4
    outs = pl.pallas_call(
        functools.partial(_rwkv_feat_kernel, n_lat=n_lat, n_rows=R, dl2=dl2, il2=il2, hs=hs),
        grid=(R // tr,),
        in_specs=[pl.BlockSpec((tr, w3), lambda i: (i, 0)), zp, zn, pl.BlockSpec((tr, lwp), lambda i: (i, 0)),
                  full(p["cw"]), full(p["w2"]), full(p["w0"]), full(p["a2"]), full(p["a0"]), full(p["g2"]),
                  full(p["k_k"]), full(p["k_a"])],
        out_specs=[out_spec] * 10,
        out_shape=[jax.ShapeDtypeStruct((R, W), F32)] * 10,
        compiler_params=_cparams(("parallel",), vm),
        name="rwkv_features",
    )(z2, z2, z2, lo, p["cw"], p["w2"], p["w0"], p["a2"], p["a0"], p["g2"], p["k_k"], p["k_a"])
    return dict(zip(("r", "v", "kn", "ld0", "ld1", "kd0", "kd1", "b0", "b1", "g"), outs))


def _scan_kernel(rf, vf, kf, ldf, kdf, bf, rb, vb, kb, ldb, kdb, bb, yf_o, yb_o, st_ref):
    @pl.when(pl.program_id(0) == 0)
    def _():
        st_ref[...] = jnp.zeros_like(st_ref)

    C, W = rf.shape
    C2 = 2 * C
    dot = functools.partial(jnp.dot, preferred_element_type=F32)
    dot_nt = lambda a, b: lax.dot_general(a, b, (((1,), (1,)), ((), ())), preferred_element_type=F32)
    ri = lax.broadcasted_iota(jnp.int32, (C, C), 0)
    ci = lax.broadcasted_iota(jnp.int32, (C, C), 1)
    ii = lax.broadcasted_iota(jnp.int32, (C2, C2), 0)
    jj = lax.broadcasted_iota(jnp.int32, (C2, C2), 1)
    im, jm = ii & (C - 1), jj & (C - 1)
    eye = jnp.where(ii == jj, 1.0, 0.0).astype(F32)
    same_head = (ii & -C) == (jj & -C)
    h0 = lax.broadcasted_iota(jnp.int32, (C, LANES), 1) < LANES // 2

    def stack(x):
        return jnp.concatenate([jnp.where(h0, x, 0.0), jnp.where(h0, 0.0, x)], axis=0)

    def fold(x):
        return x[:C] + x[C:]

    inst = []
    for d, (r_, v_, k_, ld_, kd_, b_, y_o) in enumerate(((rf, vf, kf, ldf, kdf, bf, yf_o),
                                                          (rb, vb, kb, ldb, kdb, bb, yb_o))):
        rev = d == 1
        ld, kd, b = ld_[...], kd_[...], b_[...]
        inc = jnp.where((ci >= ri) if rev else (ci <= ri), 1.0, 0.0).astype(F32)
        cum = jnp.dot(inc, ld, preferred_element_type=F32, precision=lax.Precision.HIGHEST)
        tot = cum[0:1, :] if rev else cum[C - 1:C, :]
        e_neg = jnp.exp(-cum)
        e_dec = jnp.exp(tot - cum)
        full = dict(kt=k_[...] * jnp.exp(cum - ld), rt=r_[...] * jnp.exp(cum), kh=kd * e_neg, bh=b * e_neg,
                    kdec=kd * e_dec, bdec=b * e_dec, v=v_[...], etot=jnp.exp(tot))
        strict = (jm > im) if rev else (jm < im)
        incl = (jm >= im) if rev else (jm <= im)
        for p in range(W // LANES):
            q = {n: a[:, p * LANES:(p + 1) * LANES] for n, a in full.items()}
            q.update(d=d, p=p, rev=rev, strict=strict, incl=incl, y_o=y_o)
            inst.append(q)

    for q in inst:
        lhs = jnp.concatenate([stack(q["kt"]), stack(q["rt"])], axis=0).astype(BF16)
        rhs = jnp.concatenate([stack(q["bh"]), stack(q["kh"])], axis=0).astype(BF16)
        q["G"] = dot_nt(lhs, rhs)
    for q in inst:
        G = q.pop("G")
        q["Lb"] = jnp.where(q["strict"], G[:C2, :C2], 0.0)
        q["LkAk"] = jnp.concatenate([jnp.where(q["strict"], G[:C2, C2:], 0.0),
                                     jnp.where(q["incl"], G[C2:, C2:], 0.0)], axis=0).astype(BF16)
        q["Ab"] = jnp.where(q["incl"], G[C2:, :C2], 0.0).astype(BF16)

    s = 1
    while s < C:
        same = (ii & -(2 * s)) == (jj & -(2 * s))
        hi_i, hi_j = (ii & s) != 0, (jj & s) != 0
        off = {False: same & hi_i & ~hi_j, True: same & ~hi_i & hi_j}
        if s == 1:
            for q in inst:
                q["T"] = eye - jnp.where(off[q["rev"]], q["Lb"], 0.0)
        else:
            for q in inst:
                q["Tb"] = q["T"].astype(BF16)
                q["TM"] = dot(q["Tb"], jnp.where(off[q["rev"]], q["Lb"], 0.0).astype(BF16))
            for q in inst:
                q["T"] = q["T"] - dot(q.pop("TM").astype(BF16), q.pop("Tb"))
        s *= 2

    for q in inst:
        S = st_ref[q["d"], q["p"]]
        q["S"] = S
        q["X"] = dot_nt(jnp.concatenate([q["kt"], q["rt"]], axis=0).astype(BF16), S.astype(BF16))
        q["LA"] = dot(q.pop("LkAk"), stack(q["v"]).astype(BF16))
    for q in inst:
        q["Us"] = dot(q.pop("T").astype(BF16), (stack(q["X"][:C]) + q["LA"][:C2]).astype(BF16))
    for q in inst:
        q["ABU"] = dot(q.pop("Ab"), q["Us"].astype(BF16))
    for q in inst:
        sl = slice(q["p"] * LANES, (q["p"] + 1) * LANES)
        q["y_o"][:, sl] = q["X"][C:] + fold(q["LA"][C2:] - q.pop("ABU"))
        vu_t = jnp.concatenate([q["v"], -fold(q["Us"])], axis=0).T.astype(BF16)
        upd = dot(vu_t, jnp.concatenate([q["kdec"], q["bdec"]], axis=0).astype(BF16))
        st_ref[q["d"], q["p"]] = q["S"] * q["etot"] + jnp.where(same_head, upd, 0.0)


def _rwkv_scan(f, *, n_lat):
    R, W = f["r"].shape
    C = CHUNK
    n, nl = R // C, n_lat // C
    nc = n - nl
    fwd = lambda s: (jnp.where(s < nc, nl + s, s - nc), 0)
    bwd = lambda s: (n - 1 - s, 0)
    fs, bs = pl.BlockSpec((C, W), fwd), pl.BlockSpec((C, W), bwd)
    vm = 2 * 14 * C * W * 4 + 2 * (W // LANES) * LANES * LANES * 4 + (24 << 20)
    return pl.pallas_call(
        _scan_kernel,
        grid=(n,),
        in_specs=[fs] * 6 + [bs] * 6,
        out_specs=[fs, bs],
        out_shape=[jax.ShapeDtypeStruct((R, W), F32)] * 2,
        scratch_shapes=[pltpu.VMEM((2, W // LANES, LANES, LANES), F32)],
        compiler_params=_cparams(("arbitrary",), vm),
        name="rwkv_scan",
    )(f["r"], f["v"], f["kn"], f["ld0"], f["kd0"], f["b0"], f["r"], f["v"], f["kn"], f["ld1"], f["kd1"], f["b1"])


def _rwkv_readout_kernel(yf_ref, yb_ref, r_ref, v_ref, kd0_ref, kd1_ref, g_ref, rk_ref, lw_ref, lb_ref, o_ref, *,
                         hs):
    y = yf_ref[...] + yb_ref[...]
    mu = _head_sums(y, hs) * (1.0 / hs)
    d = y - mu
    var = _head_sums(d * d, hs) * (1.0 / hs)
    yn = d * lax.rsqrt(var + GN_EPS) * lw_ref[...] + lb_ref[...]
    bonus = _head_sums(r_ref[...] * (kd0_ref[...] + kd1_ref[...]) * rk_ref[...], hs) * v_ref[...]
    o_ref[...] = ((yn + bonus) * g_ref[...]).astype(o_ref.dtype)


def _rwkv_readout(yf, yb, f, p, *, rows, hs):
    W = yf.shape[1]
    tr = ROW_TILE
    rs = pl.BlockSpec((tr, W), lambda i: (i, 0))
    ps = pl.BlockSpec((1, W), lambda i: (0, 0))
    vm = 2 * 8 * tr * W * 4 + 10 * tr * W * 4
    return pl.pallas_call(
        functools.partial(_rwkv_readout_kernel, hs=hs),
        grid=(rows // tr,),
        in_specs=[rs] * 7 + [ps] * 3,
        out_specs=rs,
        out_shape=jax.ShapeDtypeStruct((rows, W), BF16),
        compiler_params=_cparams(("parallel",), vm),
        name="rwkv_readout",
    )(yf, yb, f["r"], f["v"], f["kd0"], f["kd1"], f["g"], p["r_k"], p["ln_w"], p["ln_b"])


def _merge_kernel(a_ref, c_ref, r_ref, wa_ref, wc_ref, wr_ref, ga_ref, gc_ref, gr_ref, o_ref):
    ya = jnp.dot(a_ref[...], wa_ref[...], preferred_element_type=F32)
    yc = jnp.dot(c_ref[...], wc_ref[...], preferred_element_type=F32)
    yr = jnp.dot(r_ref[...], wr_ref[...], preferred_element_type=F32)
    o_ref[...] = (ga_ref[...].astype(F32) * ya + gc_ref[...].astype(F32) * yc
                  + gr_ref[...].astype(F32) * yr).astype(o_ref.dtype)


def _merge(att, cv, rw, wa, wc, wr, layer, gates, *, rows):
    D = wa.shape[2]
    tm = _pick(rows, (512, 256, 128, 64))
    tn = _pick(D, (1024, 512, 256, 128))
    nb = D // tn
    ka, kc, kr = wa.shape[1], wc.shape[1], wr.shape[1]
    vm = 2 * ((tm + tn) * (ka + kc + kr) * 2 + 4 * tm * tn * 2) + 4 * tm * tn * 4
    return pl.pallas_call(
        _merge_kernel,
        grid=(nb, rows // tm),
        in_specs=[
            pl.BlockSpec((tm, ka), lambda j, i: (i, 0)),
            pl.BlockSpec((tm, kc), lambda j, i: (i, 0)),
            pl.BlockSpec((tm, kr), lambda j, i: (i, 0)),
            pl.BlockSpec((None, ka, tn), lambda j, i: (layer, 0, j)),
            pl.BlockSpec((None, kc, tn), lambda j, i: (layer, 0, j)),
            pl.BlockSpec((None, kr, tn), lambda j, i: (layer, 0, j)),
            pl.BlockSpec((tm, tn), lambda j, i: (i, j)),
            pl.BlockSpec((tm, tn), lambda j, i: (i, nb + j)),
            pl.BlockSpec((tm, tn), lambda j, i: (i, 2 * nb + j)),
        ],
        out_specs=pl.BlockSpec((tm, tn), lambda j, i: (i, j)),
        out_shape=jax.ShapeDtypeStruct((rows, D), BF16),
        compiler_params=_cparams(("parallel", "parallel"), vm),
        name="merge",
    )(att, cv, rw, wa, wc, wr, gates, gates, gates)


def _rope_tables(n_lat, n_ctx, hd):
    rows = n_lat // GRID_W
    row = jnp.repeat(jnp.arange(rows, dtype=F32), GRID_W)
    col = jnp.tile(jnp.arange(GRID_W, dtype=F32), rows)
    half = hd // 2
    inv_freq = ROPE_THETA ** (-jnp.arange(0, half, 2, dtype=F32) / half)
    ar, ac = row[:, None] * inv_freq, col[:, None] * inv_freq
    z = jnp.zeros_like(ar)
    cos = jnp.concatenate([jnp.cos(ar), jnp.cos(ar), jnp.cos(ac), jnp.cos(ac)], axis=-1)
    sa = jnp.concatenate([-jnp.sin(ar), z, -jnp.sin(ac), z], axis=-1)
    sb = jnp.concatenate([z, jnp.sin(ar), z, jnp.sin(ac)], axis=-1)
    pad = lambda t, v: jnp.concatenate([t, jnp.full((n_ctx, hd), v, F32)], axis=0)
    return pad(cos, 1.0), pad(sa, 0.0), pad(sb, 0.0)


def _block_diag2(w):
    z = jnp.zeros_like(w[0])
    return jnp.concatenate([jnp.concatenate([w[0], z], axis=1), jnp.concatenate([z, w[1]], axis=1)], axis=0)


def kernel(x, c, ctx, c_ctx, w_mod, b_mod, norm1_g, norm2_g, w_in, b_gate, q_norm_g, k_norm_g, w_attn_o, conv_w, w_conv_o, rwkv_conv_w, rwkv_w0, rwkv_w2, rwkv_a0, rwkv_a2, rwkv_g2, rwkv_k_k, rwkv_k_a, rwkv_r_k, rwkv_ln_w, rwkv_ln_b, w_rwkv_o, w_out, ffn_w_up, ffn_conv_w, ffn_conv_b, ffn_w_down, final_g):
    B, S, D = x.shape
    assert B == 1, "batch is folded away; the problem fixes BATCH = 1"
    Tc = ctx.shape[1]
    R = S + Tc
    L = w_mod.shape[0]
    hd = q_norm_g.shape[1]
    aw = w_attn_o.shape[1]
    cw = conv_w.shape[2]
    W = rwkv_w0.shape[2]
    hs = rwkv_r_k.shape[2]
    dl, il, gl = rwkv_w2.shape[2], rwkv_a2.shape[2], rwkv_g2.shape[1]
    ff = ffn_w_down.shape[1]
    kvw = (w_in.shape[2] - (aw + 3 * cw + 3 * W + 2 * dl + 2 * il + gl + 3 * D)) // 2
    assert S % ROW_TILE == 0 and Tc % ROW_TILE == 0 and S % GRID_W == 0
    assert hd == LANES and 2 * hs == LANES and 2 * CHUNK == LANES and W % LANES == 0

    o_conv = aw + 2 * kvw
    o_rkv = o_conv + 3 * cw
    o_lora = o_rkv + 3 * W
    o_gate = o_lora + 2 * dl + 2 * il + gl
    lw_real = 2 * dl + 2 * il + gl
    lwp = -(-lw_real // LANES) * LANES
    gp = lwp - 2 * dl - 2 * il

    xa = jnp.concatenate([x[0], ctx[0]], axis=0)
    cond = jnp.zeros((16, D), F32).at[0].set(c[0]).at[1].set(c_ctx)
    mods = _ada_mod(cond, w_mod, b_mod)
    tabs = _rope_tables(S, Tc, hd)

    w_t = jnp.swapaxes(w_in, 1, 2)
    w_ao, w_co, w_ro, w_o = (w.astype(BF16) for w in (w_attn_o, w_conv_o, w_rwkv_o, w_out))
    g_off = o_gate - o_lora
    g_unit = _pick(o_lora, (512, 256, 128))
    g_main = (w_in.shape[2] - o_lora) // g_unit * g_unit
    g_tail = w_in.shape[2] - o_lora - g_main
    b_gate_p = jnp.pad(b_gate, ((0, 0), (g_off, 0)))
    w_gt = w_in[:, :, o_lora + g_main:].astype(BF16)
    f_tc, f_nb, f_pad = _ffn_blocks(ff)
    w_dn = jnp.pad(ffn_w_down.reshape(L, f_nb, f_tc, D), ((0, 0), (0, 0), (0, f_pad), (0, 0)))
    w_dn = w_dn.reshape(L, f_nb * (f_tc + f_pad), D).astype(BF16)

    for l in range(L):
        last = l == L - 1
        rows = S if last else R
        m = mods[l, :2].reshape(2, N_MOD, 1, D)
        mod = lambda i: m[:, i]
        rp = {
            "cw": rwkv_conv_w[l],
            "w2": _block_diag2(rwkv_w2[l]).astype(BF16),
            "w0": rwkv_w0[l].reshape(1, 2 * W),
            "a2": _block_diag2(rwkv_a2[l]).astype(BF16),
            "a0": rwkv_a0[l].reshape(1, 2 * W),
            "g2": jnp.pad(rwkv_g2[l], ((0, gp - gl), (0, 0))).astype(BF16),
            "k_k": rwkv_k_k[l].reshape(1, W),
            "k_a": rwkv_k_a[l].reshape(1, W),
            "r_k": rwkv_r_k[l].reshape(1, W),
            "ln_w": rwkv_ln_w[l].reshape(1, W),
            "ln_b": rwkv_ln_b[l].reshape(1, W),
        }

        h = _norm_mod(xa, norm1_g[l], mod(0), mod(1), S, R)
        in_proj = functools.partial(_matmul_wres, h, w_t, l, transposed=True)
        z1 = in_proj(rows=R, out_dtype=BF16, col0=0, n=o_rkv, name="in_proj_qkv_conv")
        z2 = in_proj(rows=R, out_dtype=F32, col0=o_rkv, n=3 * W, name="in_proj_rkv")
        lo = in_proj(rows=R, out_dtype=F32, col0=o_lora, n=lwp, name="in_proj_lora")
        gates = in_proj(rows=rows, out_dtype=BF16, col0=o_lora, n=g_main, bias=b_gate_p[l, :g_main],
                        name="in_proj_gates")[:, g_off:]
        if g_tail:
            gates = jnp.concatenate([gates, _matmul(h, w_gt, l, rows=rows, out_dtype=BF16, epi="sigmoid_bias",
                                                    bias=b_gate_p[l, g_main:], name="in_proj_gates_tail")], axis=1)

        qn, kn = _qk_prep(z1, tabs, q_norm_g[l], k_norm_g[l], aw, kvw, hd)
        att = _attention(qn, kn, z1, q_row0=0, q_rows=S, kv_row0=0, kv_rows=R, v_col0=aw + kvw, hd=hd)
        if not last:
            att_c = _attention(qn, kn, z1, q_row0=S, q_rows=Tc, kv_row0=S, kv_rows=Tc, v_col0=aw + kvw, hd=hd)
            att = jnp.concatenate([att, att_c], axis=0)

        cv = _conv_branch(z1, conv_w[l], rows=rows, col0=o_conv, cw=cw, n_lat=S)

        feat = _rwkv_features(z2, lo, rp, n_lat=S, hs=hs)
        yf, yb = _rwkv_scan(feat, n_lat=S)
        rw = _rwkv_readout(yf, yb, feat, rp, rows=rows, hs=hs)

        mg = _merge(att, cv, rw, w_ao, w_co, w_ro, l, gates, rows=rows)
        xa = _matmul(mg, w_o, l, rows=rows, out_dtype=F32, epi="residual", res=xa, gate2=mod(2).reshape(2, D),
                     n_lat=S, name="out_proj")

        h2 = _norm_mod(xa, norm2_g[l], mod(3), mod(4), S, rows)
        u = _matmul_wres(h2, ffn_w_up, l, rows=rows, out_dtype=BF16, name="ffn_up")
        act = _ffn_act(u, ffn_conv_w[l], ffn_conv_b[l].reshape(1, 2 * ff), rows=rows, n_lat=S)
        xa = _matmul(act, w_dn, l, rows=rows, out_dtype=F32, epi="residual", res=xa, gate2=mod(5).reshape(2, D),
                     n_lat=S, name="ffn_down")

    return _final_norm(xa, final_g, S)[None]
```

```python
import functools

import jax
import jax.numpy as jnp
from jax import lax
from jax.experimental import pallas as pl
from jax.experimental.pallas import tpu as pltpu

F32 = jnp.float32
BF16 = jnp.bfloat16

GRID_W = 64
ROPE_THETA = 10000.0
Q_PER_KV = 4
N_MOD = 6
NORM_EPS = 1e-6
GN_EPS = 64e-5
LOG2_E = 1.4426950408889634

LANES = 128
SUBLANES_F32 = 8
SUBLANES_BF16 = 16
VMEM_LIMIT_CAP = 56 * 1024 * 1024
MM_VMEM_BUDGET = 48 * 1024 * 1024
CHUNK = 64
ROW_TILE = 256


def _pick(n, candidates):
    for c in candidates:
        if n % c == 0:
            return c
    raise ValueError(f"no tile in {candidates} divides {n}")


def _cparams(sem, vmem_bytes):
    limit = int(min(max(vmem_bytes * 5 // 4 + (4 << 20), 16 << 20), VMEM_LIMIT_CAP))
    return pltpu.CompilerParams(dimension_semantics=sem, vmem_limit_bytes=limit)


def _sigmoid(x):
    return 1.0 / (1.0 + jnp.exp(-x))


def _silu(x):
    return x * _sigmoid(x)


def _softplus(x):
    return jnp.maximum(x, 0.0) + jnp.log(1.0 + jnp.exp(-jnp.abs(x)))


def _mod_kernel(c_ref, w0_ref, w1_ref, b_ref, o_ref):
    a = _silu(c_ref[...]).astype(BF16)
    h = w0_ref.shape[0]
    o_ref[...] = (jnp.dot(a[:, :h], w0_ref[...].astype(BF16), preferred_element_type=F32)
                  + jnp.dot(a[:, h:], w1_ref[...].astype(BF16), preferred_element_type=F32) + b_ref[...])


def _ada_mod(cond, w_mod, b_mod):
    L, D, N = w_mod.shape
    tn = _pick(N, (512, 256, 128))
    vm = 2 * (D * tn * 4 + 16 * tn * 4 * 2) + 16 * D * 4 * 2 + D * tn * 2
    return pl.pallas_call(
        _mod_kernel,
        grid=(L, N // tn),
        in_specs=[
            pl.BlockSpec((16, D), lambda l, j: (0, 0)),
            pl.BlockSpec((None, D // 2, tn), lambda l, j: (l, 0, j)),
            pl.BlockSpec((None, D // 2, tn), lambda l, j: (l, 1, j)),
            pl.BlockSpec((None, 1, tn), lambda l, j: (l, 0, j)),
        ],
        out_specs=pl.BlockSpec((None, 16, tn), lambda l, j: (l, 0, j)),
        out_shape=jax.ShapeDtypeStruct((L, 16, N), F32),
        compiler_params=_cparams(("parallel", "parallel"), vm),
        name="ada_mod",
    )(cond, w_mod, w_mod, b_mod.reshape(L, 1, N))


def _norm_mod_kernel(x_ref, g_ref, sh_ref, sc_ref, o_ref):
    x = x_ref[...]
    y = x * lax.rsqrt(jnp.mean(x * x, axis=-1, keepdims=True) + NORM_EPS) * g_ref[...]
    o_ref[...] = (y * (1.0 + sc_ref[...]) + sh_ref[...]).astype(o_ref.dtype)


def _norm_mod(x, g, shift2, scale2, n_lat, rows):
    D = x.shape[1]
    tr = ROW_TILE
    nl = n_lat // tr
    vm = 2 * (tr * D * 4 + tr * D * 2) + 6 * D * 4
    return pl.pallas_call(
        _norm_mod_kernel,
        grid=(rows // tr,),
        in_specs=[
            pl.BlockSpec((tr, D), lambda i: (i, 0)),
            pl.BlockSpec((1, D), lambda i: (0, 0)),
            pl.BlockSpec((None, 1, D), lambda i: (jnp.where(i >= nl, 1, 0), 0, 0)),
            pl.BlockSpec((None, 1, D), lambda i: (jnp.where(i >= nl, 1, 0), 0, 0)),
        ],
        out_specs=pl.BlockSpec((tr, D), lambda i: (i, 0)),
        out_shape=jax.ShapeDtypeStruct((rows, D), BF16),
        compiler_params=_cparams(("parallel",), vm),
        name="norm_mod",
    )(x, g.reshape(1, D), shift2, scale2)


def _final_norm_kernel(x_ref, g_ref, o_ref):
    x = x_ref[...]
    o_ref[...] = x * lax.rsqrt(jnp.mean(x * x, axis=-1, keepdims=True) + NORM_EPS) * g_ref[...]


def _final_norm(x, g, rows):
    D = x.shape[1]
    tr = ROW_TILE
    return pl.pallas_call(
        _final_norm_kernel,
        grid=(rows // tr,),
        in_specs=[pl.BlockSpec((tr, D), lambda i: (i, 0)), pl.BlockSpec((1, D), lambda i: (0, 0))],
        out_specs=pl.BlockSpec((tr, D), lambda i: (i, 0)),
        out_shape=jax.ShapeDtypeStruct((rows, D), F32),
        compiler_params=_cparams(("parallel",), 4 * tr * D * 4),
        name="final_norm",
    )(x, g.reshape(1, D))


def _mm_kernel(*refs, nk, epi, tm, n_lat):
    if nk > 1:
        acc_ref = refs[-1]
        refs = refs[:-1]
    a_ref, b_ref = refs[0], refs[1]
    o_ref = refs[-1]
    extra = refs[2:-1]
    row0 = pl.program_id(0) * tm

    def finish(acc):
        if epi == "cast":
            o_ref[...] = acc.astype(o_ref.dtype)
        elif epi == "sigmoid_bias":
            o_ref[...] = _sigmoid(acc + extra[0][...]).astype(o_ref.dtype)
        elif epi == "residual":
            res_ref, gate_ref = extra
            row = row0 + lax.broadcasted_iota(jnp.int32, acc.shape, 0)
            gate = jnp.where(row < n_lat, gate_ref[0:1, :], gate_ref[1:2, :])
            o_ref[...] = res_ref[...] + gate * acc
        else:
            raise ValueError(epi)

    part = jnp.dot(a_ref[...], b_ref[...], preferred_element_type=F32)
    if nk == 1:
        finish(part)
        return
    k = pl.program_id(2)

    @pl.when(k == 0)
    def _():
        acc_ref[...] = part

    @pl.when(k > 0)
    def _():
        acc_ref[...] += part

    @pl.when(k == nk - 1)
    def _():
        finish(acc_ref[...])


def _mm_tiles(rows, n, col0, K, osz, residual):
    best = None
    for tm in (1408, 1024, 768, 512, 256, 128, 64):
        if rows % tm:
            continue
        for tn in (2048, 1024, 512, 256, 128, n):
            if n % tn or col0 % tn:
                continue
            for tk in ((K,) if K <= 4096 else range(LANES, 5632 + 1, LANES)):
                if K % tk:
                    continue
                vm = 2 * (tm * tk * 2 + tk * tn * 2 + tm * tn * osz) + tm * tn * 4
                vm += tm * tn * 4 if tk < K else 0
                vm += 2 * tm * tn * 4 if residual else 0
                if vm > MM_VMEM_BUDGET:
                    continue
                key = (tm * tn / (tm + tn), tk)
                if best is None or key > best[0]:
                    best = (key, (tm, tn, tk), vm)
    assert best is not None
    return best[1], best[2]


def _matmul(a, b, layer, *, rows, out_dtype, col0=0, n=None, epi="cast", bias=None, res=None, gate2=None,
            n_lat=0, name="mm"):
    K = b.shape[1]
    n = b.shape[2] - col0 if n is None else n
    (tm, tn, tk), vm = _mm_tiles(rows, n, col0, K, jnp.dtype(out_dtype).itemsize, epi == "residual")
    nk = K // tk
    cb = col0 // tn
    in_specs = [
        pl.BlockSpec((tm, tk), lambda i, j, k: (i, k)),
        pl.BlockSpec((None, tk, tn), lambda i, j, k: (layer, k, cb + j)),
    ]
    args = [a, b]
    if epi == "sigmoid_bias":
        in_specs.append(pl.BlockSpec((1, tn), lambda i, j, k: (0, j)))
        args.append(bias.reshape(1, n))
    elif epi == "residual":
        in_specs.append(pl.BlockSpec((tm, tn), lambda i, j, k: (i, j)))
        in_specs.append(pl.BlockSpec((2, tn), lambda i, j, k: (0, j)))
        args += [res, gate2]
    scratch = [pltpu.VMEM((tm, tn), F32)] if nk > 1 else []
    return pl.pallas_call(
        functools.partial(_mm_kernel, nk=nk, epi=epi, tm=tm, n_lat=n_lat),
        grid=(rows // tm, n // tn, nk),
        in_specs=in_specs,
        out_specs=pl.BlockSpec((tm, tn), lambda i, j, k: (i, j)),
        out_shape=jax.ShapeDtypeStruct((rows, n), out_dtype),
        scratch_shapes=scratch,
        compiler_params=_cparams(("parallel", "parallel", "arbitrary"), vm),
        name=name,
    )(*args)


def _mm_wres_kernel(*refs, kc, transposed, sigmoid_bias):
    a_ref, w_ref = refs[0], refs[1]
    o_ref, wb_ref = refs[-2], refs[-1]

    @pl.when(pl.program_id(1) == 0)
    def _():
        for k0 in range(0, w_ref.shape[0], kc):
            wb_ref[k0:k0 + kc, :] = w_ref[k0:k0 + kc, :].astype(BF16)

    if transposed:
        acc = lax.dot_general(a_ref[...], wb_ref[...], (((1,), (1,)), ((), ())), preferred_element_type=F32)
    else:
        acc = jnp.dot(a_ref[...], wb_ref[...], preferred_element_type=F32)
    if sigmoid_bias:
        acc = _sigmoid(acc + refs[2][...])
    o_ref[...] = acc.astype(o_ref.dtype)


def _matmul_wres(a, w, layer, *, rows, out_dtype, col0=0, n=None, transposed=False, bias=None, name="mm_wres"):
    K = w.shape[2] if transposed else w.shape[1]
    n = (w.shape[1] if transposed else w.shape[2]) - col0 if n is None else n
    tn = _pick(n, tuple(t for t in (512, 256, 128) if transposed or col0 % t == 0))
    osz = jnp.dtype(out_dtype).itemsize
    for tm in (1408, 1024, 768, 512, 256, 128, 64):
        vm = 2 * (K * tn * 4 + tm * K * 2 + tm * tn * osz) + K * tn * 2 + tm * tn * 4
        if rows % tm == 0 and vm <= MM_VMEM_BUDGET:
            break
    cb = col0 // tn
    if transposed:
        assert col0 % SUBLANES_F32 == 0
        w_spec = pl.BlockSpec((None, pl.Element(tn), pl.Element(K)),
                              lambda j, i: (layer, pl.multiple_of(col0 + j * tn, SUBLANES_F32), 0))
        wb, kc = pltpu.VMEM((tn, K), BF16), _pick(tn, (128,))
    else:
        w_spec = pl.BlockSpec((None, K, tn), lambda j, i: (layer, 0, cb + j))
        wb, kc = pltpu.VMEM((K, tn), BF16), _pick(K, (512, 256, 128))
    in_specs = [pl.BlockSpec((tm, K), lambda j, i: (i, 0)), w_spec]
    args = [a, w]
    if bias is not None:
        in_specs.append(pl.BlockSpec((1, tn), lambda j, i: (0, j)))
        args.append(bias.reshape(1, n))
    return pl.pallas_call(
        functools.partial(_mm_wres_kernel, kc=kc, transposed=transposed, sigmoid_bias=bias is not None),
        grid=(n // tn, rows // tm),
        in_specs=in_specs,
        out_specs=pl.BlockSpec((tm, tn), lambda j, i: (i, j)),
        out_shape=jax.ShapeDtypeStruct((rows, n), out_dtype),
        scratch_shapes=[wb],
        compiler_params=_cparams(("arbitrary", "arbitrary"), vm),
        name=name,
    )(*args)


def _qk_prep_kernel(zq_ref, zk_ref, c_ref, sa_ref, sb_ref, gq_ref, gk_ref, q_ref, k_ref, *, hd):
    cos, sa, sb = c_ref[...], sa_ref[...], sb_ref[...]

    def norm_rope(x, g):
        y = x * lax.rsqrt(jnp.mean(x * x, axis=-1, keepdims=True) + NORM_EPS) * g
        return y * cos + pltpu.roll(y, hd - hd // 4, 1) * sa + pltpu.roll(y, hd // 4, 1) * sb

    gq = gq_ref[...] * (hd ** -0.5 * LOG2_E)
    for h in range(zq_ref.shape[1] // hd):
        sl = slice(h * hd, (h + 1) * hd)
        q_ref[:, sl] = norm_rope(zq_ref[:, sl].astype(F32), gq).astype(BF16)
    gk = gk_ref[...]
    for h in range(zk_ref.shape[1] // hd):
        sl = slice(h * hd, (h + 1) * hd)
        k_ref[:, sl] = norm_rope(zk_ref[:, sl].astype(F32), gk).astype(BF16)


def _qk_prep(z1, tabs, gq, gk, aw, kvw, hd):
    R = z1.shape[0]
    tr = ROW_TILE
    assert aw % kvw == 0
    cos, sa, sb = tabs
    tab_spec = pl.BlockSpec((tr, hd), lambda i: (i, 0))
    vm = 2 * 2 * (tr * aw * 2 + tr * kvw * 2) + 6 * tr * hd * 4 + 8 * tr * hd * 4
    return pl.pallas_call(
        functools.partial(_qk_prep_kernel, hd=hd),
        grid=(R // tr,),
        in_specs=[
            pl.BlockSpec((tr, aw), lambda i: (i, 0)),
            pl.BlockSpec((tr, kvw), lambda i: (i, aw // kvw)),
            tab_spec, tab_spec, tab_spec,
            pl.BlockSpec((1, hd), lambda i: (0, 0)),
            pl.BlockSpec((1, hd), lambda i: (0, 0)),
        ],
        out_specs=[pl.BlockSpec((tr, aw), lambda i: (i, 0)), pl.BlockSpec((tr, kvw), lambda i: (i, 0))],
        out_shape=[jax.ShapeDtypeStruct((R, aw), BF16), jax.ShapeDtypeStruct((R, kvw), BF16)],
        compiler_params=_cparams(("parallel",), vm),
        name="qk_prep",
    )(z1, z1, cos, sa, sb, gq.reshape(1, hd), gk.reshape(1, hd))


def _attn_kernel(q_ref, k_ref, v_ref, o_ref, *, hd, tk, nkv):
    tq = q_ref.shape[0]
    q4 = jnp.concatenate([q_ref[:, g * hd:(g + 1) * hd] for g in range(Q_PER_KV)], axis=0)

    ones_col = jnp.where(lax.broadcasted_iota(jnp.int32, (tk, hd), 1) == 0, 1.0, 0.0).astype(v_ref.dtype)
    m = acc = None
    for j in range(nkv):
        kc = k_ref[j * tk:(j + 1) * tk, :]
        ve = jnp.concatenate([v_ref[j * tk:(j + 1) * tk, :], ones_col], axis=1)
        s = lax.dot_general(q4, kc, (((1,), (1,)), ((), ())), preferred_element_type=F32)
        smax = jnp.max(s, axis=-1, keepdims=True)
        if j == 0:
            m = smax
            acc = jnp.dot(jnp.exp2(s - m).astype(BF16), ve, preferred_element_type=F32)
        else:
            m_new = jnp.maximum(m, smax)
            alpha = jnp.exp2(m - m_new)
            acc = alpha * acc + jnp.dot(jnp.exp2(s - m_new).astype(BF16), ve, preferred_element_type=F32)
            m = m_new
    o = acc[:, :hd] / acc[:, hd:hd + 1]
    for g in range(Q_PER_KV):
        o_ref[:, g * hd:(g + 1) * hd] = o[g * tq:(g + 1) * tq].astype(o_ref.dtype)


def _attention(q, k, z1, *, q_row0, q_rows, kv_row0, kv_rows, v_col0, hd):
    aw, kvw = q.shape[1], k.shape[1]
    n_kv = kvw // hd
    gw = Q_PER_KV * hd
    tq = _pick(q_rows, (256, 128))
    tk = _pick(kv_rows, (768, 512, 256, 128))
    assert q_row0 % tq == 0 and kv_row0 % kv_rows == 0 and v_col0 % hd == 0
    qb0, kb0, vc0 = q_row0 // tq, kv_row0 // kv_rows, v_col0 // hd
    vm = 2 * (2 * tq * gw * 2 + 2 * kv_rows * hd * 2) + 6 * Q_PER_KV * tq * tk * 4
    return pl.pallas_call(
        functools.partial(_attn_kernel, hd=hd, tk=tk, nkv=kv_rows // tk),
        grid=(n_kv, q_rows // tq),
        in_specs=[
            pl.BlockSpec((tq, gw), lambda h, i: (qb0 + i, h)),
            pl.BlockSpec((kv_rows, hd), lambda h, i: (kb0, h)),
            pl.BlockSpec((kv_rows, hd), lambda h, i: (kb0, vc0 + h)),
        ],
        out_specs=pl.BlockSpec((tq, gw), lambda h, i: (i, h)),
        out_shape=jax.ShapeDtypeStruct((q_rows, aw), BF16),
        compiler_params=_cparams(("parallel", "parallel"), vm),
        name="attention",
    )(q, k, z1)


def _halo_specs(tr, hr, width, col_block, n_rows):
    per = tr // hr
    last = n_rows // hr - 1
    prev = pl.BlockSpec((hr, width), lambda i, *_: (jnp.maximum(i * per - 1, 0), col_block(*_)))
    nxt = pl.BlockSpec((hr, width), lambda i, *_: (jnp.minimum((i + 1) * per, last), col_block(*_)))
    return prev, nxt


def _edge_flags(i, tr, n_lat, n_rows):
    r0 = i * tr
    r1 = r0 + tr
    pm = jnp.where((r0 == 0) | (r0 == n_lat), 0.0, 1.0).astype(F32)
    nm = jnp.where((r1 == n_lat) | (r1 == n_rows), 0.0, 1.0).astype(F32)
    return pm, nm


def _dwconv3(x, prev_row, next_row, w):
    tr = x.shape[0]
    rid = lax.broadcasted_iota(jnp.int32, x.shape, 0)
    xp = jnp.where(rid == 0, prev_row, pltpu.roll(x, 1, 0))
    xn = jnp.where(rid == tr - 1, next_row, pltpu.roll(x, tr - 1, 0))
    return xp * w[0:1] + x * w[1:2] + xn * w[2:3]


def _conv_branch_kernel(h_ref, hp_ref, hn_ref, b_ref, c_ref, cp_ref, cn_ref, w_ref, o_ref, *, n_lat, n_rows):
    tr = h_ref.shape[0]
    pm, nm = _edge_flags(pl.program_id(0), tr, n_lat, n_rows)
    hr = hp_ref.shape[0]
    u = c_ref[...].astype(F32) * h_ref[...].astype(F32)
    up = cp_ref[hr - 1:hr, :].astype(F32) * hp_ref[hr - 1:hr, :].astype(F32) * pm
    un = cn_ref[0:1, :].astype(F32) * hn_ref[0:1, :].astype(F32) * nm
    o_ref[...] = (b_ref[...].astype(F32) * _dwconv3(u, up, un, w_ref[...])).astype(o_ref.dtype)


def _conv_branch(z1, conv_w, *, rows, col0, cw, n_lat):
    tr, hr = ROW_TILE, SUBLANES_BF16
    assert col0 % cw == 0
    cb = col0 // cw
    main = lambda o: pl.BlockSpec((tr, cw), lambda i: (i, cb + o))
    hp, hn = _halo_specs(tr, hr, cw, lambda: cb, rows)
    cp, cn = _halo_specs(tr, hr, cw, lambda: cb + 2, rows)
    vm = 2 * 4 * tr * cw * 2 + 8 * tr * cw * 4
    return pl.pallas_call(
        functools.partial(_conv_branch_kernel, n_lat=n_lat, n_rows=rows),
        grid=(rows // tr,),
        in_specs=[main(0), hp, hn, main(1), main(2), cp, cn, pl.BlockSpec((3, cw), lambda i: (0, 0))],
        out_specs=pl.BlockSpec((tr, cw), lambda i: (i, 0)),
        out_shape=jax.ShapeDtypeStruct((rows, cw), BF16),
        compiler_params=_cparams(("parallel",), vm),
        name="conv_branch",
    )(z1, z1, z1, z1, z1, z1, z1, conv_w)


def _ffn_act_kernel(v_ref, vp_ref, vn_ref, g_ref, gp_ref, gn_ref, wv_ref, wg_ref, bv_ref, bg_ref, o_ref, *,
                    n_lat, n_rows):
    tr = v_ref.shape[0]
    pm, nm = _edge_flags(pl.program_id(0), tr, n_lat, n_rows)
    hr = vp_ref.shape[0]
    ri = lax.broadcasted_iota(jnp.int32, (2 * tr, tr), 0)
    ci = lax.broadcasted_iota(jnp.int32, (2 * tr, tr), 1)
    shift = jnp.where(ci == jnp.where(ri < tr, ri - 1, ri - tr + 1), 1.0, 0.0).astype(v_ref.dtype)
    r8 = lax.broadcasted_iota(jnp.int32, (SUBLANES_F32, v_ref.shape[1]), 0)

    def conv(x_ref, p_ref, n_ref, w_ref, b_ref):
        x = x_ref[...]
        sh = jnp.dot(shift, x, preferred_element_type=F32)
        first = sh[:SUBLANES_F32] + jnp.where(r8 == 0, p_ref[hr - 1:hr, :].astype(F32) * pm, 0.0)
        lastr = sh[2 * tr - SUBLANES_F32:] + jnp.where(r8 == SUBLANES_F32 - 1, n_ref[0:1, :].astype(F32) * nm, 0.0)
        xp = jnp.concatenate([first, sh[SUBLANES_F32:tr]], axis=0)
        xn = jnp.concatenate([sh[tr:2 * tr - SUBLANES_F32], lastr], axis=0)
        w = w_ref[...]
        return xp * w[0:1] + x.astype(F32) * w[1:2] + xn * w[2:3] + b_ref[...]

    val = conv(v_ref, vp_ref, vn_ref, wv_ref, bv_ref)
    gate = conv(g_ref, gp_ref, gn_ref, wg_ref, bg_ref)
    tc = v_ref.shape[1]
    o_ref[:, :tc] = (_silu(gate) * val).astype(o_ref.dtype)
    if o_ref.shape[1] > tc:
        o_ref[:, tc:] = jnp.zeros((tr, o_ref.shape[1] - tc), o_ref.dtype)


def _ffn_blocks(ff):
    tc = max(t for t in range(LANES, 6144 + 1, LANES) if ff % t == 0)
    nb = ff // tc
    pad = next(p for p in range(0, 1024 + 1, LANES) if (nb * (tc + p)) % 1024 == 0)
    return tc, nb, pad


def _ffn_act(u, conv_w, conv_b, *, rows, n_lat):
    ff = u.shape[1] // 2
    tr, hr = ROW_TILE // 2, SUBLANES_BF16
    tc, nb, pad = _ffn_blocks(ff)
    vp, vn = _halo_specs(tr, hr, tc, lambda j: j, rows)
    gp, gn = _halo_specs(tr, hr, tc, lambda j: nb + j, rows)
    wspec = lambda o: pl.BlockSpec((3, tc), lambda i, j: (0, o * nb + j))
    bspec = lambda o: pl.BlockSpec((1, tc), lambda i, j: (0, o * nb + j))
    vm = 2 * 3 * tr * tc * 2 + 12 * tr * tc * 4
    return pl.pallas_call(
        functools.partial(_ffn_act_kernel, n_lat=n_lat, n_rows=rows),
        grid=(rows // tr, nb),
        in_specs=[
            pl.BlockSpec((tr, tc), lambda i, j: (i, j)), vp, vn,
            pl.BlockSpec((tr, tc), lambda i, j: (i, nb + j)), gp, gn,
            wspec(0), wspec(1), bspec(0), bspec(1),
        ],
        out_specs=pl.BlockSpec((tr, tc + pad), lambda i, j: (i, j)),
        out_shape=jax.ShapeDtypeStruct((rows, nb * (tc + pad)), BF16),
        compiler_params=_cparams(("parallel", "parallel"), vm),
        name="ffn_act",
    )(u, u, u, u, u, u, conv_w, conv_w, conv_b, conv_b)


def _head_sums(x, hs):
    assert hs & (hs - 1) == 0
    li = lax.broadcasted_iota(jnp.int32, (LANES, LANES), 0) & -hs
    lj = lax.broadcasted_iota(jnp.int32, (LANES, LANES), 1) & -hs
    seg = jnp.where(li == lj, 1.0, 0.0).astype(F32)
    parts = [jnp.dot(x[:, t * LANES:(t + 1) * LANES], seg, preferred_element_type=F32,
                     precision=lax.Precision.HIGHEST) for t in range(x.shape[1] // LANES)]
    return jnp.concatenate(parts, axis=1)


def _rwkv_feat_kernel(z_ref, zp_ref, zn_ref, lo_ref, cw_ref, w2_ref, w0_ref, a2_ref, a0_ref, g2_ref, kk_ref,
                      ka_ref, r_o, v_o, kn_o, ld0_o, ld1_o, kd0_o, kd1_o, b0_o, b1_o, g_o, *,
                      n_lat, n_rows, dl2, il2, hs):
    tr, w3 = z_ref.shape
    W = w3 // 3
    pm, nm = _edge_flags(pl.program_id(0), tr, n_lat, n_rows)
    hr = zp_ref.shape[0]
    rkv = _dwconv3(z_ref[...], zp_ref[hr - 1:hr, :] * pm, zn_ref[0:1, :] * nm, cw_ref[...])
    r, k, v = rkv[:, :W], rkv[:, W:2 * W], rkv[:, 2 * W:]
    lw = lo_ref[:, :dl2]
    la = lo_ref[:, dl2:dl2 + il2]
    lg = lo_ref[:, dl2 + il2:]
    wl = jnp.dot(jnp.tanh(lw).astype(BF16), w2_ref[...], preferred_element_type=F32) + w0_ref[...]
    ld = -jnp.exp(-_softplus(-wl) - 0.5)
    a = _sigmoid(jnp.dot(la.astype(BF16), a2_ref[...], preferred_element_type=F32) + a0_ref[...])
    g = jnp.dot(_sigmoid(lg).astype(BF16), g2_ref[...], preferred_element_type=F32)
    kk = k * kk_ref[...]
    kn = kk / jnp.maximum(jnp.sqrt(_head_sums(kk * kk, hs)), 1e-12)
    ka = ka_ref[...]
    r_o[...] = r
    v_o[...] = v
    kn_o[...] = kn
    g_o[...] = g
    for d, (ld_o, kd_o, b_o) in enumerate(((ld0_o, kd0_o, b0_o), (ld1_o, kd1_o, b1_o))):
        ad = a[:, d * W:(d + 1) * W]
        ld_o[...] = ld[:, d * W:(d + 1) * W]
        kd_o[...] = k * (1.0 + (ad - 1.0) * ka)
        b_o[...] = kn * ad


def _rwkv_features(z2, lo, p, *, n_lat, hs):
    R, w3 = z2.shape
    W = w3 // 3
    lwp = lo.shape[1]
    tr, hr = ROW_TILE // 2, SUBLANES_F32
    dl2, il2 = p["w2"].shape[0], p["a2"].shape[0]
    zp, zn = _halo_specs(tr, hr, w3, lambda: 0, R)
    full = lambda a: pl.BlockSpec(a.shape, lambda i: (0,) * a.ndim)
    out_spec = pl.BlockSpec((tr, W), lambda i: (i, 0))
    vm = 2 * (tr * w3 * 4 + tr * lwp * 4 + 10 * tr * W * 4) + 12 * tr * w3 * 4
    outs = pl.pallas_call(
        functools.partial(_rwkv_feat_kernel, n_lat=n_lat, n_rows=R, dl2=dl2, il2=il2, hs=hs),
        grid=(R // tr,),
        in_specs=[pl.BlockSpec((tr, w3), lambda i: (i, 0)), zp, zn, pl.BlockSpec((tr, lwp), lambda i: (i, 0)),
                  full(p["cw"]), full(p["w2"]), full(p["w0"]), full(p["a2"]), full(p["a0"]), full(p["g2"]),
                  full(p["k_k"]), full(p["k_a"])],
        out_specs=[out_spec] * 10,
        out_shape=[jax.ShapeDtypeStruct((R, W), F32)] * 10,
        compiler_params=_cparams(("parallel",), vm),
        name="rwkv_features",
    )(z2, z2, z2, lo, p["cw"], p["w2"], p["w0"], p["a2"], p["a0"], p["g2"], p["k_k"], p["k_a"])
    return dict(zip(("r", "v", "kn", "ld0", "ld1", "kd0", "kd1", "b0", "b1", "g"), outs))


def _scan_kernel(rf, vf, kf, ldf, kdf, bf, rb, vb, kb, ldb, kdb, bb, yf_o, yb_o, st_ref):
    @pl.when(pl.program_id(0) == 0)
    def _():
        st_ref[...] = jnp.zeros_like(st_ref)

    C, W = rf.shape
    C2 = 2 * C
    dot = functools.partial(jnp.dot, preferred_element_type=F32)
    dot_nt = lambda a, b: lax.dot_general(a, b, (((1,), (1,)), ((), ())), preferred_element_type=F32)
    ri = lax.broadcasted_iota(jnp.int32, (C, C), 0)
    ci = lax.broadcasted_iota(jnp.int32, (C, C), 1)
    ii = lax.broadcasted_iota(jnp.int32, (C2, C2), 0)
    jj = lax.broadcasted_iota(jnp.int32, (C2, C2), 1)
    im, jm = ii & (C - 1), jj & (C - 1)
    eye = jnp.where(ii == jj, 1.0, 0.0).astype(F32)
    same_head = (ii & -C) == (jj & -C)
    h0 = lax.broadcasted_iota(jnp.int32, (C, LANES), 1) < LANES // 2

    def stack(x):
        return jnp.concatenate([jnp.where(h0, x, 0.0), jnp.where(h0, 0.0, x)], axis=0)

    def fold(x):
        return x[:C] + x[C:]

    inst = []
    for d, (r_, v_, k_, ld_, kd_, b_, y_o) in enumerate(((rf, vf, kf, ldf, kdf, bf, yf_o),
                                                          (rb, vb, kb, ldb, kdb, bb, yb_o))):
        rev = d == 1
        ld, kd, b = ld_[...], kd_[...], b_[...]
        inc = jnp.where((ci >= ri) if rev else (ci <= ri), 1.0, 0.0).astype(F32)
        cum = jnp.dot(inc, ld, preferred_element_type=F32, precision=lax.Precision.HIGHEST)
        tot = cum[0:1, :] if rev else cum[C - 1:C, :]
        e_neg = jnp.exp(-cum)
        e_dec = jnp.exp(tot - cum)
        full = dict(kt=k_[...] * jnp.exp(cum - ld), rt=r_[...] * jnp.exp(cum), kh=kd * e_neg, bh=b * e_neg,
                    kdec=kd * e_dec, bdec=b * e_dec, v=v_[...], etot=jnp.exp(tot))
        strict = (jm > im) if rev else (jm < im)
        incl = (jm >= im) if rev else (jm <= im)
        for p in range(W // LANES):
            q = {n: a[:, p * LANES:(p + 1) * LANES] for n, a in full.items()}
            q.update(d=d, p=p, rev=rev, strict=strict, incl=incl, y_o=y_o)
            inst.append(q)

    for q in inst:
        lhs = jnp.concatenate([stack(q["kt"]), stack(q["rt"])], axis=0).astype(BF16)
        rhs = jnp.concatenate([stack(q["bh"]), stack(q["kh"])], axis=0).astype(BF16)
        q["G"] = dot_nt(lhs, rhs)
    for q in inst:
        G = q.pop("G")
        q["Lb"] = jnp.where(q["strict"], G[:C2, :C2], 0.0)
        q["LkAk"] = jnp.concatenate([jnp.where(q["strict"], G[:C2, C2:], 0.0),
                                     jnp.where(q["incl"], G[C2:, C2:], 0.0)], axis=0).astype(BF16)
        q["Ab"] = jnp.where(q["incl"], G[C2:, :C2], 0.0).astype(BF16)

    s = 1
    while s < C:
        same = (ii & -(2 * s)) == (jj & -(2 * s))
        hi_i, hi_j = (ii & s) != 0, (jj & s) != 0
        off = {False: same & hi_i & ~hi_j, True: same & ~hi_i & hi_j}
        if s == 1:
            for q in inst:
                q["T"] = eye - jnp.where(off[q["rev"]], q["Lb"], 0.0)
        else:
            for q in inst:
                q["Tb"] = q["T"].astype(BF16)
                q["TM"] = dot(q["Tb"], jnp.where(off[q["rev"]], q["Lb"], 0.0).astype(BF16))
            for q in inst:
                q["T"] = q["T"] - dot(q.pop("TM").astype(BF16), q.pop("Tb"))
        s *= 2

    for q in inst:
        S = st_ref[q["d"], q["p"]]
        q["S"] = S
        q["X"] = dot_nt(jnp.concatenate([q["kt"], q["rt"]], axis=0).astype(BF16), S.astype(BF16))
        q["LA"] = dot(q.pop("LkAk"), stack(q["v"]).astype(BF16))
    for q in inst:
        q["Us"] = dot(q.pop("T").astype(BF16), (stack(q["X"][:C]) + q["LA"][:C2]).astype(BF16))
    for q in inst:
        q["ABU"] = dot(q.pop("Ab"), q["Us"].astype(BF16))
    for q in inst:
        sl = slice(q["p"] * LANES, (q["p"] + 1) * LANES)
        q["y_o"][:, sl] = q["X"][C:] + fold(q["LA"][C2:] - q.pop("ABU"))
        vu_t = jnp.concatenate([q["v"], -fold(q["Us"])], axis=0).T.astype(BF16)
        upd = dot(vu_t, jnp.concatenate([q["kdec"], q["bdec"]], axis=0).astype(BF16))
        st_ref[q["d"], q["p"]] = q["S"] * q["etot"] + jnp.where(same_head, upd, 0.0)


def _rwkv_scan(f, *, n_lat):
    R, W = f["r"].shape
    C = CHUNK
    n, nl = R // C, n_lat // C
    nc = n - nl
    fwd = lambda s: (jnp.where(s < nc, nl + s, s - nc), 0)
    bwd = lambda s: (n - 1 - s, 0)
    fs, bs = pl.BlockSpec((C, W), fwd), pl.BlockSpec((C, W), bwd)
    vm = 2 * 14 * C * W * 4 + 2 * (W // LANES) * LANES * LANES * 4 + (24 << 20)
    return pl.pallas_call(
        _scan_kernel,
        grid=(n,),
        in_specs=[fs] * 6 + [bs] * 6,
        out_specs=[fs, bs],
        out_shape=[jax.ShapeDtypeStruct((R, W), F32)] * 2,
        scratch_shapes=[pltpu.VMEM((2, W // LANES, LANES, LANES), F32)],
        compiler_params=_cparams(("arbitrary",), vm),
        name="rwkv_scan",
    )(f["r"], f["v"], f["kn"], f["ld0"], f["kd0"], f["b0"], f["r"], f["v"], f["kn"], f["ld1"], f["kd1"], f["b1"])


def _rwkv_readout_kernel(yf_ref, yb_ref, r_ref, v_ref, kd0_ref, kd1_ref, g_ref, rk_ref, lw_ref, lb_ref, o_ref, *,
                         hs):
    y = yf_ref[...] + yb_ref[...]
    mu = _head_sums(y, hs) * (1.0 / hs)
    d = y - mu
    var = _head_sums(d * d, hs) * (1.0 / hs)
    yn = d * lax.rsqrt(var + GN_EPS) * lw_ref[...] + lb_ref[...]
    bonus = _head_sums(r_ref[...] * (kd0_ref[...] + kd1_ref[...]) * rk_ref[...], hs) * v_ref[...]
    o_ref[...] = ((yn + bonus) * g_ref[...]).astype(o_ref.dtype)


def _rwkv_readout(yf, yb, f, p, *, rows, hs):
    W = yf.shape[1]
    tr = ROW_TILE
    rs = pl.BlockSpec((tr, W), lambda i: (i, 0))
    ps = pl.BlockSpec((1, W), lambda i: (0, 0))
    vm = 2 * 8 * tr * W * 4 + 10 * tr * W * 4
    return pl.pallas_call(
        functools.partial(_rwkv_readout_kernel, hs=hs),
        grid=(rows // tr,),
        in_specs=[rs] * 7 + [ps] * 3,
        out_specs=rs,
        out_shape=jax.ShapeDtypeStruct((rows, W), BF16),
        compiler_params=_cparams(("parallel",), vm),
        name="rwkv_readout",
    )(yf, yb, f["r"], f["v"], f["kd0"], f["kd1"], f["g"], p["r_k"], p["ln_w"], p["ln_b"])


def _merge_kernel(a_ref, c_ref, r_ref, wa_ref, wc_ref, wr_ref, ga_ref, gc_ref, gr_ref, o_ref):
    ya = jnp.dot(a_ref[...], wa_ref[...], preferred_element_type=F32)
    yc = jnp.dot(c_ref[...], wc_ref[...], preferred_element_type=F32)
    yr = jnp.dot(r_ref[...], wr_ref[...], preferred_element_type=F32)
    o_ref[...] = (ga_ref[...].astype(F32) * ya + gc_ref[...].astype(F32) * yc
                  + gr_ref[...].astype(F32) * yr).astype(o_ref.dtype)


def _merge(att, cv, rw, wa, wc, wr, layer, gates, *, rows):
    D = wa.shape[2]
    tm = _pick(rows, (512, 256, 128, 64))
    tn = _pick(D, (1024, 512, 256, 128))
    nb = D // tn
    ka, kc, kr = wa.shape[1], wc.shape[1], wr.shape[1]
    vm = 2 * ((tm + tn) * (ka + kc + kr) * 2 + 4 * tm * tn * 2) + 4 * tm * tn * 4
    return pl.pallas_call(
        _merge_kernel,
        grid=(nb, rows // tm),
        in_specs=[
            pl.BlockSpec((tm, ka), lambda j, i: (i, 0)),
            pl.BlockSpec((tm, kc), lambda j, i: (i, 0)),
            pl.BlockSpec((tm, kr), lambda j, i: (i, 0)),
            pl.BlockSpec((None, ka, tn), lambda j, i: (layer, 0, j)),
            pl.BlockSpec((None, kc, tn), lambda j, i: (layer, 0, j)),
            pl.BlockSpec((None, kr, tn), lambda j, i: (layer, 0, j)),
            pl.BlockSpec((tm, tn), lambda j, i: (i, j)),
            pl.BlockSpec((tm, tn), lambda j, i: (i, nb + j)),
            pl.BlockSpec((tm, tn), lambda j, i: (i, 2 * nb + j)),
        ],
        out_specs=pl.BlockSpec((tm, tn), lambda j, i: (i, j)),
        out_shape=jax.ShapeDtypeStruct((rows, D), BF16),
        compiler_params=_cparams(("parallel", "parallel"), vm),
        name="merge",
    )(att, cv, rw, wa, wc, wr, gates, gates, gates)


def _rope_tables(n_lat, n_ctx, hd):
    rows = n_lat // GRID_W
    row = jnp.repeat(jnp.arange(rows, dtype=F32), GRID_W)
    col = jnp.tile(jnp.arange(GRID_W, dtype=F32), rows)
    half = hd // 2
    inv_freq = ROPE_THETA ** (-jnp.arange(0, half, 2, dtype=F32) / half)
    ar, ac = row[:, None] * inv_freq, col[:, None] * inv_freq
    z = jnp.zeros_like(ar)
    cos = jnp.concatenate([jnp.cos(ar), jnp.cos(ar), jnp.cos(ac), jnp.cos(ac)], axis=-1)
    sa = jnp.concatenate([-jnp.sin(ar), z, -jnp.sin(ac), z], axis=-1)
    sb = jnp.concatenate([z, jnp.sin(ar), z, jnp.sin(ac)], axis=-1)
    pad = lambda t, v: jnp.concatenate([t, jnp.full((n_ctx, hd), v, F32)], axis=0)
    return pad(cos, 1.0), pad(sa, 0.0), pad(sb, 0.0)


def _block_diag2(w):
    z = jnp.zeros_like(w[0])
    return jnp.concatenate([jnp.concatenate([w[0], z], axis=1), jnp.concatenate([z, w[1]], axis=1)], axis=0)


def kernel(x, c, ctx, c_ctx, w_mod, b_mod, norm1_g, norm2_g, w_in, b_gate, q_norm_g, k_norm_g, w_attn_o, conv_w, w_conv_o, rwkv_conv_w, rwkv_w0, rwkv_w2, rwkv_a0, rwkv_a2, rwkv_g2, rwkv_k_k, rwkv_k_a, rwkv_r_k, rwkv_ln_w, rwkv_ln_b, w_rwkv_o, w_out, ffn_w_up, ffn_conv_w, ffn_conv_b, ffn_w_down, final_g):
    B, S, D = x.shape
    assert B == 1, "batch is folded away; the problem fixes BATCH = 1"
    Tc = ctx.shape[1]
    R = S + Tc
    L = w_mod.shape[0]
    hd = q_norm_g.shape[1]
    aw = w_attn_o.shape[1]
    cw = conv_w.shape[2]
    W = rwkv_w0.shape[2]
    hs = rwkv_r_k.shape[2]
    dl, il, gl = rwkv_w2.shape[2], rwkv_a2.shape[2], rwkv_g2.shape[1]
    ff = ffn_w_down.shape[1]
    kvw = (w_in.shape[2] - (aw + 3 * cw + 3 * W + 2 * dl + 2 * il + gl + 3 * D)) // 2
    assert S % ROW_TILE == 0 and Tc % ROW_TILE == 0 and S % GRID_W == 0
    assert hd == LANES and 2 * hs == LANES and 2 * CHUNK == LANES and W % LANES == 0

    o_conv = aw + 2 * kvw
    o_rkv = o_conv + 3 * cw
    o_lora = o_rkv + 3 * W
    o_gate = o_lora + 2 * dl + 2 * il + gl
    lw_real = 2 * dl + 2 * il + gl
    lwp = -(-lw_real // LANES) * LANES
    gp = lwp - 2 * dl - 2 * il

    xa = jnp.concatenate([x[0], ctx[0]], axis=0)
    cond = jnp.zeros((16, D), F32).at[0].set(c[0]).at[1].set(c_ctx)
    mods = _ada_mod(cond, w_mod, b_mod)
    tabs = _rope_tables(S, Tc, hd)

    w_t = jnp.swapaxes(w_in, 1, 2)
    w_ao, w_co, w_ro, w_o = (w.astype(BF16) for w in (w_attn_o, w_conv_o, w_rwkv_o, w_out))
    f_tc, f_nb, f_pad = _ffn_blocks(ff)
    w_dn = jnp.pad(ffn_w_down.reshape(L, f_nb, f_tc, D), ((0, 0), (0, 0), (0, f_pad), (0, 0)))
    w_dn = w_dn.reshape(L, f_nb * (f_tc + f_pad), D).astype(BF16)

    for l in range(L):
        last = l == L - 1
        rows = S if last else R
        m = mods[l, :2].reshape(2, N_MOD, 1, D)
        mod = lambda i: m[:, i]
        rp = {
            "cw": rwkv_conv_w[l],
            "w2": _block_diag2(rwkv_w2[l]).astype(BF16),
            "w0": rwkv_w0[l].reshape(1, 2 * W),
            "a2": _block_diag2(rwkv_a2[l]).astype(BF16),
            "a0": rwkv_a0[l].reshape(1, 2 * W),
            "g2": jnp.pad(rwkv_g2[l], ((0, gp - gl), (0, 0))).astype(BF16),
            "k_k": rwkv_k_k[l].reshape(1, W),
            "k_a": rwkv_k_a[l].reshape(1, W),
            "r_k": rwkv_r_k[l].reshape(1, W),
            "ln_w": rwkv_ln_w[l].reshape(1, W),
            "ln_b": rwkv_ln_b[l].reshape(1, W),
        }

        h = _norm_mod(xa, norm1_g[l], mod(0), mod(1), S, R)
        in_proj = functools.partial(_matmul_wres, h, w_t, l, transposed=True)
        z1 = in_proj(rows=R, out_dtype=BF16, col0=0, n=o_rkv, name="in_proj_qkv_conv")
        z2 = in_proj(rows=R, out_dtype=F32, col0=o_rkv, n=3 * W, name="in_proj_rkv")
        lo = in_proj(rows=R, out_dtype=F32, col0=o_lora, n=lwp, name="in_proj_lora")
        gates = in_proj(rows=rows, out_dtype=BF16, col0=o_gate, n=3 * D, bias=b_gate[l], name="in_proj_gates")

        qn, kn = _qk_prep(z1, tabs, q_norm_g[l], k_norm_g[l], aw, kvw, hd)
        att = _attention(qn, kn, z1, q_row0=0, q_rows=S, kv_row0=0, kv_rows=R, v_col0=aw + kvw, hd=hd)
        if not last:
            att_c = _attention(qn, kn, z1, q_row0=S, q_rows=Tc, kv_row0=S, kv_rows=Tc, v_col0=aw + kvw, hd=hd)
            att = jnp.concatenate([att, att_c], axis=0)

        cv = _conv_branch(z1, conv_w[l], rows=rows, col0=o_conv, cw=cw, n_lat=S)

        feat = _rwkv_features(z2, lo, rp, n_lat=S, hs=hs)
        yf, yb = _rwkv_scan(feat, n_lat=S)
        rw = _rwkv_readout(yf, yb, feat, rp, rows=rows, hs=hs)

        mg = _merge(att, cv, rw, w_ao, w_co, w_ro, l, gates, rows=rows)
        xa = _matmul(mg, w_o, l, rows=rows, out_dtype=F32, epi="residual", res=xa, gate2=mod(2).reshape(2, D),
                     n_lat=S, name="out_proj")

        h2 = _norm_mod(xa, norm2_g[l], mod(3), mod(4), S, rows)
        u = _matmul_wres(h2, ffn_w_up, l, rows=rows, out_dtype=BF16, name="ffn_up")
        act = _ffn_act(u, ffn_conv_w[l], ffn_conv_b[l].reshape(1, 2 * ff), rows=rows, n_lat=S)
        xa = _matmul(act, w_dn, l, rows=rows, out_dtype=F32, epi="residual", res=xa, gate2=mod(5).reshape(2, D),
                     n_lat=S, name="ffn_down")

    return _final_norm(xa, final_g, S)[None]
```

```python
import functools

import jax
import jax.numpy as jnp
from jax import lax
from jax.experimental import pallas as pl
from jax.experimental.pallas import tpu as pltpu

F32 = jnp.float32
BF16 = jnp.bfloat16

GRID_W = 64
ROPE_THETA = 10000.0
Q_PER_KV = 4
N_MOD = 6
NORM_EPS = 1e-6
GN_EPS = 64e-5
LOG2_E = 1.4426950408889634

LANES = 128
SUBLANES_F32 = 8
SUBLANES_BF16 = 16
VMEM_LIMIT_CAP = 56 * 1024 * 1024
MM_VMEM_BUDGET = 48 * 1024 * 1024
VMEM_LIMIT_FLOOR = 16 * 1024 * 1024
MXU_DIM = 256
MAX_FULL_K = 16 * MXU_DIM
MAX_TK = 22 * MXU_DIM
MAX_FFN_COLS = 24 * MXU_DIM
FFN_K_UNIT = 4 * MXU_DIM
CHUNK = 64
ROW_TILE = 256


def _pick(n, candidates):
    for c in candidates:
        if n % c == 0:
            return c
    raise ValueError(f"no tile in {candidates} divides {n}")


def _cparams(sem, vmem_bytes):
    limit = int(min(max(vmem_bytes * 5 // 4, VMEM_LIMIT_FLOOR), VMEM_LIMIT_CAP))
    return pltpu.CompilerParams(dimension_semantics=sem, vmem_limit_bytes=limit)


def _sigmoid(x):
    return 1.0 / (1.0 + jnp.exp(-x))


def _silu(x):
    return x * _sigmoid(x)


def _softplus(x):
    return jnp.maximum(x, 0.0) + jnp.log(1.0 + jnp.exp(-jnp.abs(x)))


def _mod_kernel(c_ref, w_ref, b_ref, o_ref):
    a = _silu(c_ref[...]).astype(BF16)
    o_ref[...] = jnp.dot(a, w_ref[...].astype(BF16), preferred_element_type=F32) + b_ref[...]


def _ada_mod(cond, w_mod, b_mod):
    L, D, N = w_mod.shape
    tn = _pick(N, (512, 256, 128))
    vm = 2 * (D * tn * 4 + 16 * tn * 4 * 2) + 16 * D * 4 * 2 + D * tn * 2
    return pl.pallas_call(
        _mod_kernel,
        grid=(L, N // tn),
        in_specs=[
            pl.BlockSpec((16, D), lambda l, j: (0, 0)),
            pl.BlockSpec((None, D, tn), lambda l, j: (l, 0, j)),
            pl.BlockSpec((None, 1, tn), lambda l, j: (l, 0, j)),
        ],
        out_specs=pl.BlockSpec((None, 16, tn), lambda l, j: (l, 0, j)),
        out_shape=jax.ShapeDtypeStruct((L, 16, N), F32),
        compiler_params=_cparams(("parallel", "parallel"), vm),
        name="ada_mod",
    )(cond, w_mod, b_mod.reshape(L, 1, N))


def _norm_mod_kernel(x_ref, g_ref, sh_ref, sc_ref, o_ref):
    x = x_ref[...]
    y = x * lax.rsqrt(jnp.mean(x * x, axis=-1, keepdims=True) + NORM_EPS) * g_ref[...]
    o_ref[...] = (y * (1.0 + sc_ref[...]) + sh_ref[...]).astype(o_ref.dtype)


def _norm_mod(x, g, shift2, scale2, n_lat, rows):
    D = x.shape[1]
    tr = ROW_TILE
    nl = n_lat // tr
    vm = 2 * (tr * D * 4 + tr * D * 2) + 6 * D * 4
    return pl.pallas_call(
        _norm_mod_kernel,
        grid=(rows // tr,),
        in_specs=[
            pl.BlockSpec((tr, D), lambda i: (i, 0)),
            pl.BlockSpec((1, D), lambda i: (0, 0)),
            pl.BlockSpec((None, 1, D), lambda i: (jnp.where(i >= nl, 1, 0), 0, 0)),
            pl.BlockSpec((None, 1, D), lambda i: (jnp.where(i >= nl, 1, 0), 0, 0)),
        ],
        out_specs=pl.BlockSpec((tr, D), lambda i: (i, 0)),
        out_shape=jax.ShapeDtypeStruct((rows, D), BF16),
        compiler_params=_cparams(("parallel",), vm),
        name="norm_mod",
    )(x, g.reshape(1, D), shift2, scale2)


def _final_norm_kernel(x_ref, g_ref, o_ref):
    x = x_ref[...]
    o_ref[...] = x * lax.rsqrt(jnp.mean(x * x, axis=-1, keepdims=True) + NORM_EPS) * g_ref[...]


def _final_norm(x, g, rows):
    D = x.shape[1]
    tr = ROW_TILE
    return pl.pallas_call(
        _final_norm_kernel,
        grid=(rows // tr,),
        in_specs=[pl.BlockSpec((tr, D), lambda i: (i, 0)), pl.BlockSpec((1, D), lambda i: (0, 0))],
        out_specs=pl.BlockSpec((tr, D), lambda i: (i, 0)),
        out_shape=jax.ShapeDtypeStruct((rows, D), F32),
        compiler_params=_cparams(("parallel",), 4 * tr * D * 4),
        name="final_norm",
    )(x, g.reshape(1, D))


def _mm_kernel(*refs, nk, epi, tm, n_lat, transposed):
    if nk > 1:
        acc_ref = refs[-1]
        refs = refs[:-1]
    a_ref, w_ref = refs[0], refs[1]
    o_ref = refs[-1]
    extra = refs[2:-1]
    row0 = pl.program_id(0) * tm

    def finish(acc):
        if epi == "cast":
            o_ref[...] = acc.astype(o_ref.dtype)
        elif epi == "sigmoid_bias":
            o_ref[...] = _sigmoid(acc + extra[0][...]).astype(o_ref.dtype)
        elif epi == "residual":
            res_ref, gate_ref = extra
            row = row0 + lax.broadcasted_iota(jnp.int32, acc.shape, 0)
            gate = jnp.where(row < n_lat, gate_ref[0:1, :], gate_ref[1:2, :])
            o_ref[...] = res_ref[...] + gate * acc
        else:
            raise ValueError(epi)

    w = w_ref[...].astype(BF16)
    if transposed:
        part = lax.dot_general(a_ref[...], w, (((1,), (1,)), ((), ())), preferred_element_type=F32)
    else:
        part = jnp.dot(a_ref[...], w, preferred_element_type=F32)
    if nk == 1:
        finish(part)
        return
    k = pl.program_id(2)

    @pl.when(k == 0)
    def _():
        acc_ref[...] = part

    @pl.when(k > 0)
    def _():
        acc_ref[...] += part

    @pl.when(k == nk - 1)
    def _():
        finish(acc_ref[...])


def _mm_tiles(rows, n, col0, K, osz, wsz, residual, any_col0):
    best = None
    for tm in (1408, 1024, 768, 512, 256, 128, 64):
        if rows % tm:
            continue
        for tn in (2048, 1024, 512, 256, 128):
            if n % tn or (col0 % tn and not any_col0):
                continue
            for tk in ((K,) if K <= MAX_FULL_K else range(LANES, MAX_TK + 1, LANES)):
                if K % tk:
                    continue
                vm = 2 * (tm * tk * 2 + tk * tn * wsz + tm * tn * osz) + tm * tn * 4
                vm += tk * tn * 2 if wsz > 2 else 0
                vm += tm * tn * 4 if tk < K else 0
                vm += 2 * tm * tn * 4 if residual else 0
                if vm > MM_VMEM_BUDGET:
                    continue
                key = (tm * tn / (tm + tn), tk)
                if best is None or key > best[0]:
                    best = (key, (tm, tn, tk), vm)
    assert best is not None
    return best[1], best[2]


def _matmul(a, w, layer, *, rows, out_dtype, col0=0, n=None, transposed=False, epi="cast", bias=None, res=None,
            gate2=None, n_lat=0, name="mm"):
    K = w.shape[2] if transposed else w.shape[1]
    n = (w.shape[1] if transposed else w.shape[2]) - col0 if n is None else n
    (tm, tn, tk), vm = _mm_tiles(rows, n, col0, K, jnp.dtype(out_dtype).itemsize, w.dtype.itemsize,
                                 epi == "residual", transposed)
    nk = K // tk
    if transposed:
        assert nk == 1 and col0 % SUBLANES_F32 == 0
        w_spec = pl.BlockSpec((None, pl.Element(tn), pl.Element(K)),
                              lambda i, j, k: (layer, pl.multiple_of(col0 + j * tn, SUBLANES_F32), 0))
    else:
        cb = col0 // tn
        w_spec = pl.BlockSpec((None, tk, tn), lambda i, j, k: (layer, k, cb + j))
    in_specs = [pl.BlockSpec((tm, tk), lambda i, j, k: (i, k)), w_spec]
    args = [a, w]
    if epi == "sigmoid_bias":
        in_specs.append(pl.BlockSpec((1, tn), lambda i, j, k: (0, j)))
        args.append(bias.reshape(1, n))
    elif epi == "residual":
        in_specs.append(pl.BlockSpec((tm, tn), lambda i, j, k: (i, j)))
        in_specs.append(pl.BlockSpec((2, tn), lambda i, j, k: (0, j)))
        args += [res, gate2]
    scratch = [pltpu.VMEM((tm, tn), F32)] if nk > 1 else []
    return pl.pallas_call(
        functools.partial(_mm_kernel, nk=nk, epi=epi, tm=tm, n_lat=n_lat, transposed=transposed),
        grid=(rows // tm, n // tn, nk),
        in_specs=in_specs,
        out_specs=pl.BlockSpec((tm, tn), lambda i, j, k: (i, j)),
        out_shape=jax.ShapeDtypeStruct((rows, n), out_dtype),
        scratch_shapes=scratch,
        compiler_params=_cparams(("parallel", "parallel", "arbitrary"), vm),
        name=name,
    )(*args)


def _qk_prep_kernel(zq_ref, zk_ref, c_ref, sa_ref, sb_ref, gq_ref, gk_ref, q_ref, k_ref, *, hd):
    cos, sa, sb = c_ref[...], sa_ref[...], sb_ref[...]

    def norm_rope(x, g):
        y = x * lax.rsqrt(jnp.mean(x * x, axis=-1, keepdims=True) + NORM_EPS) * g
        return y * cos + pltpu.roll(y, hd - hd // 4, 1) * sa + pltpu.roll(y, hd // 4, 1) * sb

    gq = gq_ref[...] * (hd ** -0.5 * LOG2_E)
    for h in range(zq_ref.shape[1] // hd):
        sl = slice(h * hd, (h + 1) * hd)
        q_ref[:, sl] = norm_rope(zq_ref[:, sl].astype(F32), gq).astype(BF16)
    gk = gk_ref[...]
    for h in range(zk_ref.shape[1] // hd):
        sl = slice(h * hd, (h + 1) * hd)
        k_ref[:, sl] = norm_rope(zk_ref[:, sl].astype(F32), gk).astype(BF16)


def _qk_prep(z1, tabs, gq, gk, aw, kvw, hd):
    R = z1.shape[0]
    tr = ROW_TILE
    assert aw % kvw == 0
    cos, sa, sb = tabs
    tab_spec = pl.BlockSpec((tr, hd), lambda i: (i, 0))
    vm = 2 * 2 * (tr * aw * 2 + tr * kvw * 2) + 6 * tr * hd * 4 + 8 * tr * hd * 4
    return pl.pallas_call(
        functools.partial(_qk_prep_kernel, hd=hd),
        grid=(R // tr,),
        in_specs=[
            pl.BlockSpec((tr, aw), lambda i: (i, 0)),
            pl.BlockSpec((tr, kvw), lambda i: (i, aw // kvw)),
            tab_spec, tab_spec, tab_spec,
            pl.BlockSpec((1, hd), lambda i: (0, 0)),
            pl.BlockSpec((1, hd), lambda i: (0, 0)),
        ],
        out_specs=[pl.BlockSpec((tr, aw), lambda i: (i, 0)), pl.BlockSpec((tr, kvw), lambda i: (i, 0))],
        out_shape=[jax.ShapeDtypeStruct((R, aw), BF16), jax.ShapeDtypeStruct((R, kvw), BF16)],
        compiler_params=_cparams(("parallel",), vm),
        name="qk_prep",
    )(z1, z1, cos, sa, sb, gq.reshape(1, hd), gk.reshape(1, hd))


def _attn_kernel(q_ref, k_ref, v_ref, o_ref, *, hd, tk, nkv):
    tq = q_ref.shape[0]
    q4 = jnp.concatenate([q_ref[:, g * hd:(g + 1) * hd] for g in range(Q_PER_KV)], axis=0)

    ones_col = jnp.where(lax.broadcasted_iota(jnp.int32, (tk, hd), 1) == 0, 1.0, 0.0).astype(v_ref.dtype)
    m = acc = None
    for j in range(nkv):
        kc = k_ref[j * tk:(j + 1) * tk, :]
        ve = jnp.concatenate([v_ref[j * tk:(j + 1) * tk, :], ones_col], axis=1)
        s = lax.dot_general(q4, kc, (((1,), (1,)), ((), ())), preferred_element_type=F32)
        smax = jnp.max(s, axis=-1, keepdims=True)
        if j == 0:
            m = smax
            acc = jnp.dot(jnp.exp2(s - m).astype(BF16), ve, preferred_element_type=F32)
        else:
            m_new = jnp.maximum(m, smax)
            alpha = jnp.exp2(m - m_new)
            acc = alpha * acc + jnp.dot(jnp.exp2(s - m_new).astype(BF16), ve, preferred_element_type=F32)
            m = m_new
    o = acc[:, :hd] / acc[:, hd:hd + 1]
    for g in range(Q_PER_KV):
        o_ref[:, g * hd:(g + 1) * hd] = o[g * tq:(g + 1) * tq].astype(o_ref.dtype)


def _attention(q, k, z1, *, q_row0, q_rows, kv_row0, kv_rows, v_col0, hd):
    aw, kvw = q.shape[1], k.shape[1]
    n_kv = kvw // hd
    gw = Q_PER_KV * hd
    tq = _pick(q_rows, (256, 128))
    tk = _pick(kv_rows, (768, 512, 256, 128))
    assert q_row0 % tq == 0 and kv_row0 % kv_rows == 0 and v_col0 % hd == 0
    qb0, kb0, vc0 = q_row0 // tq, kv_row0 // kv_rows, v_col0 // hd
    vm = 2 * (2 * tq * gw * 2 + 2 * kv_rows * hd * 2) + 6 * Q_PER_KV * tq * tk * 4
    return pl.pallas_call(
        functools.partial(_attn_kernel, hd=hd, tk=tk, nkv=kv_rows // tk),
        grid=(n_kv, q_rows // tq),
        in_specs=[
            pl.BlockSpec((tq, gw), lambda h, i: (qb0 + i, h)),
            pl.BlockSpec((kv_rows, hd), lambda h, i: (kb0, h)),
            pl.BlockSpec((kv_rows, hd), lambda h, i: (kb0, vc0 + h)),
        ],
        out_specs=pl.BlockSpec((tq, gw), lambda h, i: (i, h)),
        out_shape=jax.ShapeDtypeStruct((q_rows, aw), BF16),
        compiler_params=_cparams(("parallel", "parallel"), vm),
        name="attention",
    )(q, k, z1)


def _halo_specs(tr, hr, width, col_block, n_rows):
    per = tr // hr
    last = n_rows // hr - 1
    prev = pl.BlockSpec((hr, width), lambda i, *_: (jnp.maximum(i * per - 1, 0), col_block(*_)))
    nxt = pl.BlockSpec((hr, width), lambda i, *_: (jnp.minimum((i + 1) * per, last), col_block(*_)))
    return prev, nxt


def _edge_flags(i, tr, n_lat, n_rows):
    r0 = i * tr
    r1 = r0 + tr
    pm = jnp.where((r0 == 0) | (r0 == n_lat), 0.0, 1.0).astype(F32)
    nm = jnp.where((r1 == n_lat) | (r1 == n_rows), 0.0, 1.0).astype(F32)
    return pm, nm


def _dwconv3(x, prev_row, next_row, w):
    tr = x.shape[0]
    rid = lax.broadcasted_iota(jnp.int32, x.shape, 0)
    xp = jnp.where(rid == 0, prev_row, pltpu.roll(x, 1, 0))
    xn = jnp.where(rid == tr - 1, next_row, pltpu.roll(x, tr - 1, 0))
    return xp * w[0:1] + x * w[1:2] + xn * w[2:3]


def _conv_branch_kernel(h_ref, hp_ref, hn_ref, b_ref, c_ref, cp_ref, cn_ref, w_ref, o_ref, *, n_lat, n_rows):
    tr = h_ref.shape[0]
    pm, nm = _edge_flags(pl.program_id(0), tr, n_lat, n_rows)
    hr = hp_ref.shape[0]
    u = c_ref[...].astype(F32) * h_ref[...].astype(F32)
    up = cp_ref[hr - 1:hr, :].astype(F32) * hp_ref[hr - 1:hr, :].astype(F32) * pm
    un = cn_ref[0:1, :].astype(F32) * hn_ref[0:1, :].astype(F32) * nm
    o_ref[...] = (b_ref[...].astype(F32) * _dwconv3(u, up, un, w_ref[...])).astype(o_ref.dtype)


def _conv_branch(z1, conv_w, *, rows, col0, cw, n_lat):
    tr, hr = ROW_TILE, SUBLANES_BF16
    assert col0 % cw == 0
    cb = col0 // cw
    main = lambda o: pl.BlockSpec((tr, cw), lambda i: (i, cb + o))
    hp, hn = _halo_specs(tr, hr, cw, lambda: cb, rows)
    cp, cn = _halo_specs(tr, hr, cw, lambda: cb + 2, rows)
    vm = 2 * 4 * tr * cw * 2 + 8 * tr * cw * 4
    return pl.pallas_call(
        functools.partial(_conv_branch_kernel, n_lat=n_lat, n_rows=rows),
        grid=(rows // tr,),
        in_specs=[main(0), hp, hn, main(1), main(2), cp, cn, pl.BlockSpec((3, cw), lambda i: (0, 0))],
        out_specs=pl.BlockSpec((tr, cw), lambda i: (i, 0)),
        out_shape=jax.ShapeDtypeStruct((rows, cw), BF16),
        compiler_params=_cparams(("parallel",), vm),
        name="conv_branch",
    )(z1, z1, z1, z1, z1, z1, z1, conv_w)


def _ffn_act_kernel(v_ref, vp_ref, vn_ref, g_ref, gp_ref, gn_ref, wv_ref, wg_ref, bv_ref, bg_ref, o_ref, *,
                    n_lat, n_rows):
    tr = v_ref.shape[0]
    pm, nm = _edge_flags(pl.program_id(0), tr, n_lat, n_rows)
    hr = vp_ref.shape[0]
    ri = lax.broadcasted_iota(jnp.int32, (2 * tr, tr), 0)
    ci = lax.broadcasted_iota(jnp.int32, (2 * tr, tr), 1)
    shift = jnp.where(ci == jnp.where(ri < tr, ri - 1, ri - tr + 1), 1.0, 0.0).astype(v_ref.dtype)
    r8 = lax.broadcasted_iota(jnp.int32, (SUBLANES_F32, v_ref.shape[1]), 0)

    def conv(x_ref, p_ref, n_ref, w_ref, b_ref):
        x = x_ref[...]
        sh = jnp.dot(shift, x, preferred_element_type=F32)
        first = sh[:SUBLANES_F32] + jnp.where(r8 == 0, p_ref[hr - 1:hr, :].astype(F32) * pm, 0.0)
        lastr = sh[2 * tr - SUBLANES_F32:] + jnp.where(r8 == SUBLANES_F32 - 1, n_ref[0:1, :].astype(F32) * nm, 0.0)
        xp = jnp.concatenate([first, sh[SUBLANES_F32:tr]], axis=0)
        xn = jnp.concatenate([sh[tr:2 * tr - SUBLANES_F32], lastr], axis=0)
        w = w_ref[...]
        return xp * w[0:1] + x.astype(F32) * w[1:2] + xn * w[2:3] + b_ref[...]

    val = conv(v_ref, vp_ref, vn_ref, wv_ref, bv_ref)
    gate = conv(g_ref, gp_ref, gn_ref, wg_ref, bg_ref)
    tc = v_ref.shape[1]
    o_ref[:, :tc] = (_silu(gate) * val).astype(o_ref.dtype)
    if o_ref.shape[1] > tc:
        o_ref[:, tc:] = jnp.zeros((tr, o_ref.shape[1] - tc), o_ref.dtype)


def _ffn_blocks(ff):
    tc = max(t for t in range(LANES, MAX_FFN_COLS + 1, LANES) if ff % t == 0)
    nb = ff // tc
    pad = next(p for p in range(0, FFN_K_UNIT + 1, LANES) if (nb * (tc + p)) % FFN_K_UNIT == 0)
    return tc, nb, pad


def _ffn_act(u, conv_w, conv_b, *, rows, n_lat):
    ff = u.shape[1] // 2
    tr, hr = ROW_TILE // 2, SUBLANES_BF16
    tc, nb, pad = _ffn_blocks(ff)
    vp, vn = _halo_specs(tr, hr, tc, lambda j: j, rows)
    gp, gn = _halo_specs(tr, hr, tc, lambda j: nb + j, rows)
    wspec = lambda o: pl.BlockSpec((3, tc), lambda i, j: (0, o * nb + j))
    bspec = lambda o: pl.BlockSpec((1, tc), lambda i, j: (0, o * nb + j))
    vm = 2 * 3 * tr * tc * 2 + 12 * tr * tc * 4
    return pl.pallas_call(
        functools.partial(_ffn_act_kernel, n_lat=n_lat, n_rows=rows),
        grid=(rows // tr, nb),
        in_specs=[
            pl.BlockSpec((tr, tc), lambda i, j: (i, j)), vp, vn,
            pl.BlockSpec((tr, tc), lambda i, j: (i, nb + j)), gp, gn,
            wspec(0), wspec(1), bspec(0), bspec(1),
        ],
        out_specs=pl.BlockSpec((tr, tc + pad), lambda i, j: (i, j)),
        out_shape=jax.ShapeDtypeStruct((rows, nb * (tc + pad)), BF16),
        compiler_params=_cparams(("parallel", "parallel"), vm),
        name="ffn_act",
    )(u, u, u, u, u, u, conv_w, conv_w, conv_b, conv_b)


def _head_sums(x, hs):
    assert hs & (hs - 1) == 0
    li = lax.broadcasted_iota(jnp.int32, (LANES, LANES), 0) & -hs
    lj = lax.broadcasted_iota(jnp.int32, (LANES, LANES), 1) & -hs
    seg = jnp.where(li == lj, 1.0, 0.0).astype(F32)
    parts = [jnp.dot(x[:, t * LANES:(t + 1) * LANES], seg, preferred_element_type=F32,
                     precision=lax.Precision.HIGHEST) for t in range(x.shape[1] // LANES)]
    return jnp.concatenate(parts, axis=1)


def _rwkv_feat_kernel(z_ref, zp_ref, zn_ref, lo_ref, cw_ref, w2_ref, w0_ref, a2_ref, a0_ref, g2_ref, kk_ref,
                      ka_ref, r_o, v_o, kn_o, ld0_o, ld1_o, kd0_o, kd1_o, b0_o, b1_o, g_o, *,
                      n_lat, n_rows, dl2, il2, hs):
    tr, w3 = z_ref.shape
    W = w3 // 3
    pm, nm = _edge_flags(pl.program_id(0), tr, n_lat, n_rows)
    hr = zp_ref.shape[0]
    rkv = _dwconv3(z_ref[...], zp_ref[hr - 1:hr, :] * pm, zn_ref[0:1, :] * nm, cw_ref[...])
    r, k, v = rkv[:, :W], rkv[:, W:2 * W], rkv[:, 2 * W:]
    lw = lo_ref[:, :dl2]
    la = lo_ref[:, dl2:dl2 + il2]
    lg = lo_ref[:, dl2 + il2:]
    wl = jnp.dot(jnp.tanh(lw).astype(BF16), w2_ref[...], preferred_element_type=F32) + w0_ref[...]
    ld = -jnp.exp(-_softplus(-wl) - 0.5)
    a = _sigmoid(jnp.dot(la.astype(BF16), a2_ref[...], preferred_element_type=F32) + a0_ref[...])
    g = jnp.dot(_sigmoid(lg).astype(BF16), g2_ref[...], preferred_element_type=F32)
    kk = k * kk_ref[...]
    kn = kk / jnp.maximum(jnp.sqrt(_head_sums(kk * kk, hs)), 1e-12)
    ka = ka_ref[...]
    r_o[...] = r
    v_o[...] = v
    kn_o[...] = kn
    g_o[...] = g
    for d, (ld_o, kd_o, b_o) in enumerate(((ld0_o, kd0_o, b0_o), (ld1_o, kd1_o, b1_o))):
        ad = a[:, d * W:(d + 1) * W]
        ld_o[...] = ld[:, d * W:(d + 1) * W]
        kd_o[...] = k * (1.0 + (ad - 1.0) * ka)
        b_o[...] = kn * ad


def _rwkv_features(z2, lo, p, *, n_lat, hs):
    R, w3 = z2.shape
    W = w3 // 3
    lwp = lo.shape[1]
    tr, hr = ROW_TILE // 2, SUBLANES_F32
    dl2, il2 = p["w2"].shape[0], p["a2"].shape[0]
    zp, zn = _halo_specs(tr, hr, w3, lambda: 0, R)
    full = lambda a: pl.BlockSpec(a.shape, lambda i: (0,) * a.ndim)
    out_spec = pl.BlockSpec((tr, W), lambda i: (i, 0))
    vm = 2 * (tr * w3 * 4 + tr * lwp * 4 + 10 * tr * W * 4) + 12 * tr * w3 * 4
    outs = pl.pallas_call(
        functools.partial(_rwkv_feat_kernel, n_lat=n_lat, n_rows=R, dl2=dl2, il2=il2, hs=hs),
        grid=(R // tr,),
        in_specs=[pl.BlockSpec((tr, w3), lambda i: (i, 0)), zp, zn, pl.BlockSpec((tr, lwp), lambda i: (i, 0)),
                  full(p["cw"]), full(p["w2"]), full(p["w0"]), full(p["a2"]), full(p["a0"]), full(p["g2"]),
                  full(p["k_k"]), full(p["k_a"])],
        out_specs=[out_spec] * 10,
        out_shape=[jax.ShapeDtypeStruct((R, W), F32)] * 10,
        compiler_params=_cparams(("parallel",), vm),
        name="rwkv_features",
    )(z2, z2, z2, lo, p["cw"], p["w2"], p["w0"], p["a2"], p["a0"], p["g2"], p["k_k"], p["k_a"])
    return dict(zip(("r", "v", "kn", "ld0", "ld1", "kd0", "kd1", "b0", "b1", "g"), outs))


def _scan_kernel(rf, vf, kf, ldf, kdf, bf, rb, vb, kb, ldb, kdb, bb, yf_o, yb_o, st_ref):
    @pl.when(pl.program_id(0) == 0)
    def _():
        st_ref[...] = jnp.zeros_like(st_ref)

    C, W = rf.shape
    C2 = 2 * C
    dot = functools.partial(jnp.dot, preferred_element_type=F32)
    dot_nt = lambda a, b: lax.dot_general(a, b, (((1,), (1,)), ((), ())), preferred_element_type=F32)
    ri = lax.broadcasted_iota(jnp.int32, (C, C), 0)
    ci = lax.broadcasted_iota(jnp.int32, (C, C), 1)
    ii = lax.broadcasted_iota(jnp.int32, (C2, C2), 0)
    jj = lax.broadcasted_iota(jnp.int32, (C2, C2), 1)
    im, jm = ii & (C - 1), jj & (C - 1)
    eye = jnp.where(ii == jj, 1.0, 0.0).astype(F32)
    same_head = (ii & -C) == (jj & -C)
    h0 = lax.broadcasted_iota(jnp.int32, (C, LANES), 1) < LANES // 2

    def stack(x):
        return jnp.concatenate([jnp.where(h0, x, 0.0), jnp.where(h0, 0.0, x)], axis=0)

    def fold(x):
        return x[:C] + x[C:]

    inst = []
    for d, (r_, v_, k_, ld_, kd_, b_, y_o) in enumerate(((rf, vf, kf, ldf, kdf, bf, yf_o),
                                                          (rb, vb, kb, ldb, kdb, bb, yb_o))):
        rev = d == 1
        ld, kd, b = ld_[...], kd_[...], b_[...]
        inc = jnp.where((ci >= ri) if rev else (ci <= ri), 1.0, 0.0).astype(F32)
        cum = jnp.dot(inc, ld, preferred_element_type=F32, precision=lax.Precision.HIGHEST)
        tot = cum[0:1, :] if rev else cum[C - 1:C, :]
        e_neg = jnp.exp(-cum)
        e_dec = jnp.exp(tot - cum)
        full = dict(kt=k_[...] * jnp.exp(cum - ld), rt=r_[...] * jnp.exp(cum), kh=kd * e_neg, bh=b * e_neg,
                    kdec=kd * e_dec, bdec=b * e_dec, v=v_[...], etot=jnp.exp(tot))
        strict = (jm > im) if rev else (jm < im)
        incl = (jm >= im) if rev else (jm <= im)
        for p in range(W // LANES):
            q = {n: a[:, p * LANES:(p + 1) * LANES] for n, a in full.items()}
            q.update(d=d, p=p, rev=rev, strict=strict, incl=incl, y_o=y_o)
            inst.append(q)

    for q in inst:
        lhs = jnp.concatenate([stack(q["kt"]), stack(q["rt"])], axis=0).astype(BF16)
        rhs = jnp.concatenate([stack(q["bh"]), stack(q["kh"])], axis=0).astype(BF16)
        q["G"] = dot_nt(lhs, rhs)
    for q in inst:
        G = q.pop("G")
        q["Lb"] = jnp.where(q["strict"], G[:C2, :C2], 0.0)
        q["LkAk"] = jnp.concatenate([jnp.where(q["strict"], G[:C2, C2:], 0.0),
                                     jnp.where(q["incl"], G[C2:, C2:], 0.0)], axis=0).astype(BF16)
        q["Ab"] = jnp.where(q["incl"], G[C2:, :C2], 0.0).astype(BF16)

    s = 1
    while s < C:
        same = (ii & -(2 * s)) == (jj & -(2 * s))
        hi_i, hi_j = (ii & s) != 0, (jj & s) != 0
        off = {False: same & hi_i & ~hi_j, True: same & ~hi_i & hi_j}
        if s == 1:
            for q in inst:
                q["T"] = eye - jnp.where(off[q["rev"]], q["Lb"], 0.0)
        else:
            for q in inst:
                q["Tb"] = q["T"].astype(BF16)
                q["TM"] = dot(q["Tb"], jnp.where(off[q["rev"]], q["Lb"], 0.0).astype(BF16))
            for q in inst:
                q["T"] = q["T"] - dot(q.pop("TM").astype(BF16), q.pop("Tb"))
        s *= 2

    for q in inst:
        S = st_ref[q["d"], q["p"]]
        q["S"] = S
        q["X"] = dot_nt(jnp.concatenate([q["kt"], q["rt"]], axis=0).astype(BF16), S.astype(BF16))
        q["LA"] = dot(q.pop("LkAk"), stack(q["v"]).astype(BF16))
    for q in inst:
        q["Us"] = dot(q.pop("T").astype(BF16), (stack(q["X"][:C]) + q["LA"][:C2]).astype(BF16))
    for q in inst:
        q["ABU"] = dot(q.pop("Ab"), q["Us"].astype(BF16))
    for q in inst:
        sl = slice(q["p"] * LANES, (q["p"] + 1) * LANES)
        q["y_o"][:, sl] = q["X"][C:] + fold(q["LA"][C2:] - q.pop("ABU"))
        vu_t = jnp.concatenate([q["v"], -fold(q["Us"])], axis=0).T.astype(BF16)
        upd = dot(vu_t, jnp.concatenate([q["kdec"], q["bdec"]], axis=0).astype(BF16))
        st_ref[q["d"], q["p"]] = q["S"] * q["etot"] + jnp.where(same_head, upd, 0.0)


def _rwkv_scan(f, *, n_lat):
    R, W = f["r"].shape
    C = CHUNK
    n, nl = R // C, n_lat // C
    nc = n - nl
    fwd = lambda s: (jnp.where(s < nc, nl + s, s - nc), 0)
    bwd = lambda s: (n - 1 - s, 0)
    fs, bs = pl.BlockSpec((C, W), fwd), pl.BlockSpec((C, W), bwd)
    inst_tmp = 24 * LANES * LANES * 4
    vm = 2 * 14 * C * W * 4 + 2 * (W // LANES) * (LANES * LANES * 4 + inst_tmp)
    return pl.pallas_call(
        _scan_kernel,
        grid=(n,),
        in_specs=[fs] * 6 + [bs] * 6,
        out_specs=[fs, bs],
        out_shape=[jax.ShapeDtypeStruct((R, W), F32)] * 2,
        scratch_shapes=[pltpu.VMEM((2, W // LANES, LANES, LANES), F32)],
        compiler_params=_cparams(("arbitrary",), vm),
        name="rwkv_scan",
    )(f["r"], f["v"], f["kn"], f["ld0"], f["kd0"], f["b0"], f["r"], f["v"], f["kn"], f["ld1"], f["kd1"], f["b1"])


def _rwkv_readout_kernel(yf_ref, yb_ref, r_ref, v_ref, kd0_ref, kd1_ref, g_ref, rk_ref, lw_ref, lb_ref, o_ref, *,
                         hs):
    y = yf_ref[...] + yb_ref[...]
    mu = _head_sums(y, hs) * (1.0 / hs)
    d = y - mu
    var = _head_sums(d * d, hs) * (1.0 / hs)
    yn = d * lax.rsqrt(var + GN_EPS) * lw_ref[...] + lb_ref[...]
    bonus = _head_sums(r_ref[...] * (kd0_ref[...] + kd1_ref[...]) * rk_ref[...], hs) * v_ref[...]
    o_ref[...] = ((yn + bonus) * g_ref[...]).astype(o_ref.dtype)


def _rwkv_readout(yf, yb, f, p, *, rows, hs):
    W = yf.shape[1]
    tr = ROW_TILE
    rs = pl.BlockSpec((tr, W), lambda i: (i, 0))
    ps = pl.BlockSpec((1, W), lambda i: (0, 0))
    vm = 2 * 8 * tr * W * 4 + 10 * tr * W * 4
    return pl.pallas_call(
        functools.partial(_rwkv_readout_kernel, hs=hs),
        grid=(rows // tr,),
        in_specs=[rs] * 7 + [ps] * 3,
        out_specs=rs,
        out_shape=jax.ShapeDtypeStruct((rows, W), BF16),
        compiler_params=_cparams(("parallel",), vm),
        name="rwkv_readout",
    )(yf, yb, f["r"], f["v"], f["kd0"], f["kd1"], f["g"], p["r_k"], p["ln_w"], p["ln_b"])


def _merge_kernel(a_ref, c_ref, r_ref, wa_ref, wc_ref, wr_ref, ga_ref, gc_ref, gr_ref, o_ref):
    ya = jnp.dot(a_ref[...], wa_ref[...], preferred_element_type=F32)
    yc = jnp.dot(c_ref[...], wc_ref[...], preferred_element_type=F32)
    yr = jnp.dot(r_ref[...], wr_ref[...], preferred_element_type=F32)
    o_ref[...] = (ga_ref[...].astype(F32) * ya + gc_ref[...].astype(F32) * yc
                  + gr_ref[...].astype(F32) * yr).astype(o_ref.dtype)


def _merge(att, cv, rw, wa, wc, wr, layer, gates, *, rows):
    D = wa.shape[2]
    tm = _pick(rows, (512, 256, 128, 64))
    tn = _pick(D, (1024, 512, 256, 128))
    nb = D // tn
    ka, kc, kr = wa.shape[1], wc.shape[1], wr.shape[1]
    vm = 2 * ((tm + tn) * (ka + kc + kr) * 2 + 4 * tm * tn * 2) + 4 * tm * tn * 4
    return pl.pallas_call(
        _merge_kernel,
        grid=(nb, rows // tm),
        in_specs=[
            pl.BlockSpec((tm, ka), lambda j, i: (i, 0)),
            pl.BlockSpec((tm, kc), lambda j, i: (i, 0)),
            pl.BlockSpec((tm, kr), lambda j, i: (i, 0)),
            pl.BlockSpec((None, ka, tn), lambda j, i: (layer, 0, j)),
            pl.BlockSpec((None, kc, tn), lambda j, i: (layer, 0, j)),
            pl.BlockSpec((None, kr, tn), lambda j, i: (layer, 0, j)),
            pl.BlockSpec((tm, tn), lambda j, i: (i, j)),
            pl.BlockSpec((tm, tn), lambda j, i: (i, nb + j)),
            pl.BlockSpec((tm, tn), lambda j, i: (i, 2 * nb + j)),
        ],
        out_specs=pl.BlockSpec((tm, tn), lambda j, i: (i, j)),
        out_shape=jax.ShapeDtypeStruct((rows, D), BF16),
        compiler_params=_cparams(("parallel", "parallel"), vm),
        name="merge",
    )(att, cv, rw, wa, wc, wr, gates, gates, gates)


def _rope_tables(n_lat, n_ctx, hd):
    rows = n_lat // GRID_W
    row = jnp.repeat(jnp.arange(rows, dtype=F32), GRID_W)
    col = jnp.tile(jnp.arange(GRID_W, dtype=F32), rows)
    half = hd // 2
    inv_freq = ROPE_THETA ** (-jnp.arange(0, half, 2, dtype=F32) / half)
    ar, ac = row[:, None] * inv_freq, col[:, None] * inv_freq
    z = jnp.zeros_like(ar)
    cos = jnp.concatenate([jnp.cos(ar), jnp.cos(ar), jnp.cos(ac), jnp.cos(ac)], axis=-1)
    sa = jnp.concatenate([-jnp.sin(ar), z, -jnp.sin(ac), z], axis=-1)
    sb = jnp.concatenate([z, jnp.sin(ar), z, jnp.sin(ac)], axis=-1)
    pad = lambda t, v: jnp.concatenate([t, jnp.full((n_ctx, hd), v, F32)], axis=0)
    return pad(cos, 1.0), pad(sa, 0.0), pad(sb, 0.0)


def _block_diag2(w):
    z = jnp.zeros_like(w[0])
    return jnp.concatenate([jnp.concatenate([w[0], z], axis=1), jnp.concatenate([z, w[1]], axis=1)], axis=0)


def kernel(x, c, ctx, c_ctx, w_mod, b_mod, norm1_g, norm2_g, w_in, b_gate, q_norm_g, k_norm_g, w_attn_o, conv_w, w_conv_o, rwkv_conv_w, rwkv_w0, rwkv_w2, rwkv_a0, rwkv_a2, rwkv_g2, rwkv_k_k, rwkv_k_a, rwkv_r_k, rwkv_ln_w, rwkv_ln_b, w_rwkv_o, w_out, ffn_w_up, ffn_conv_w, ffn_conv_b, ffn_w_down, final_g):
    B, S, D = x.shape
    assert B == 1, "batch is folded away; the problem fixes BATCH = 1"
    Tc = ctx.shape[1]
    R = S + Tc
    L = w_mod.shape[0]
    hd = q_norm_g.shape[1]
    aw = w_attn_o.shape[1]
    cw = conv_w.shape[2]
    W = rwkv_w0.shape[2]
    hs = rwkv_r_k.shape[2]
    dl, il, gl = rwkv_w2.shape[2], rwkv_a2.shape[2], rwkv_g2.shape[1]
    ff = ffn_w_down.shape[1]
    kvw = (w_in.shape[2] - (aw + 3 * cw + 3 * W + 2 * dl + 2 * il + gl + 3 * D)) // 2
    assert S % ROW_TILE == 0 and Tc % ROW_TILE == 0 and S % GRID_W == 0
    assert hd == LANES and 2 * hs == LANES and 2 * CHUNK == LANES and W % LANES == 0

    o_conv = aw + 2 * kvw
    o_rkv = o_conv + 3 * cw
    o_lora = o_rkv + 3 * W
    o_gate = o_lora + 2 * dl + 2 * il + gl
    lw_real = 2 * dl + 2 * il + gl
    lwp = -(-lw_real // LANES) * LANES
    gp = lwp - 2 * dl - 2 * il

    xa = jnp.concatenate([x[0], ctx[0]], axis=0)
    cond = jnp.zeros((16, D), F32).at[0].set(c[0]).at[1].set(c_ctx)
    mods = _ada_mod(cond, w_mod, b_mod)
    tabs = _rope_tables(S, Tc, hd)

    w_t = jnp.swapaxes(w_in, 1, 2)
    w_ao, w_co, w_ro = (w.astype(BF16) for w in (w_attn_o, w_conv_o, w_rwkv_o))
    f_tc, f_nb, f_pad = _ffn_blocks(ff)
    w_dn = jnp.pad(ffn_w_down.reshape(L, f_nb, f_tc, D), ((0, 0), (0, 0), (0, f_pad), (0, 0)))
    w_dn = w_dn.reshape(L, f_nb * (f_tc + f_pad), D).astype(BF16)

    for l in range(L):
        last = l == L - 1
        rows = S if last else R
        m = mods[l, :2].reshape(2, N_MOD, 1, D)
        mod = lambda i: m[:, i]
        rp = {
            "cw": rwkv_conv_w[l],
            "w2": _block_diag2(rwkv_w2[l]).astype(BF16),
            "w0": rwkv_w0[l].reshape(1, 2 * W),
            "a2": _block_diag2(rwkv_a2[l]).astype(BF16),
            "a0": rwkv_a0[l].reshape(1, 2 * W),
            "g2": jnp.pad(rwkv_g2[l], ((0, gp - gl), (0, 0))).astype(BF16),
            "k_k": rwkv_k_k[l].reshape(1, W),
            "k_a": rwkv_k_a[l].reshape(1, W),
            "r_k": rwkv_r_k[l].reshape(1, W),
            "ln_w": rwkv_ln_w[l].reshape(1, W),
            "ln_b": rwkv_ln_b[l].reshape(1, W),
        }

        h = _norm_mod(xa, norm1_g[l], mod(0), mod(1), S, R)
        in_proj = functools.partial(_matmul, h, w_t, l, transposed=True)
        z1 = in_proj(rows=R, out_dtype=BF16, col0=0, n=o_rkv, name="in_proj_qkv_conv")
        z2 = in_proj(rows=R, out_dtype=F32, col0=o_rkv, n=3 * W, name="in_proj_rkv")
        lo = in_proj(rows=R, out_dtype=F32, col0=o_lora, n=lwp, name="in_proj_lora")
        gates = in_proj(rows=rows, out_dtype=BF16, col0=o_gate, n=3 * D, epi="sigmoid_bias", bias=b_gate[l],
                        name="in_proj_gates")

        qn, kn = _qk_prep(z1, tabs, q_norm_g[l], k_norm_g[l], aw, kvw, hd)
        att = _attention(qn, kn, z1, q_row0=0, q_rows=S, kv_row0=0, kv_rows=R, v_col0=aw + kvw, hd=hd)
        if not last:
            att_c = _attention(qn, kn, z1, q_row0=S, q_rows=Tc, kv_row0=S, kv_rows=Tc, v_col0=aw + kvw, hd=hd)
            att = jnp.concatenate([att, att_c], axis=0)

        cv = _conv_branch(z1, conv_w[l], rows=rows, col0=o_conv, cw=cw, n_lat=S)

        feat = _rwkv_features(z2, lo, rp, n_lat=S, hs=hs)
        yf, yb = _rwkv_scan(feat, n_lat=S)
        rw = _rwkv_readout(yf, yb, feat, rp, rows=rows, hs=hs)

        mg = _merge(att, cv, rw, w_ao, w_co, w_ro, l, gates, rows=rows)
        xa = _matmul(mg, w_out, l, rows=rows, out_dtype=F32, epi="residual", res=xa, gate2=mod(2).reshape(2, D),
                     n_lat=S, name="out_proj")

        h2 = _norm_mod(xa, norm2_g[l], mod(3), mod(4), S, rows)
        u = _matmul(h2, ffn_w_up, l, rows=rows, out_dtype=BF16, name="ffn_up")
        act = _ffn_act(u, ffn_conv_w[l], ffn_conv_b[l].reshape(1, 2 * ff), rows=rows, n_lat=S)
        xa = _matmul(act, w_dn, l, rows=rows, out_dtype=F32, epi="residual", res=xa, gate2=mod(5).reshape(2, D),
                     n_lat=S, name="ffn_down")

    return _final_norm(xa, final_g, S)[None]
```

```python
import functools

import jax
import jax.numpy as jnp
from jax import lax
from jax.experimental import pallas as pl
from jax.experimental.pallas import tpu as pltpu

F32 = jnp.float32
BF16 = jnp.bfloat16

GRID_W = 64
ROPE_THETA = 10000.0
Q_PER_KV = 4
N_MOD = 6
NORM_EPS = 1e-6
GN_EPS = 64e-5
LOG2_E = 1.4426950408889634

LANES = 128
SUBLANES_F32 = 8
SUBLANES_BF16 = 16
VMEM_LIMIT_CAP = 56 * 1024 * 1024
MM_VMEM_BUDGET = 48 * 1024 * 1024
VMEM_LIMIT_FLOOR = 16 * 1024 * 1024
MM_INTENSITY_ENOUGH = 384
MXU_DIM = 256
MAX_FULL_K = 16 * MXU_DIM
MAX_TK = 22 * MXU_DIM
MAX_FFN_COLS = 24 * MXU_DIM
FFN_K_UNIT = 4 * MXU_DIM
CHUNK = 64
ROW_TILE = 256


def _pick(n, candidates):
    for c in candidates:
        if n % c == 0:
            return c
    raise ValueError(f"no tile in {candidates} divides {n}")


def _cparams(sem, vmem_bytes):
    limit = int(min(max(vmem_bytes * 5 // 4, VMEM_LIMIT_FLOOR), VMEM_LIMIT_CAP))
    return pltpu.CompilerParams(dimension_semantics=sem, vmem_limit_bytes=limit)


def _sigmoid(x):
    return 1.0 / (1.0 + jnp.exp(-x))


def _silu(x):
    return x * _sigmoid(x)


def _softplus(x):
    return jnp.maximum(x, 0.0) + jnp.log(1.0 + jnp.exp(-jnp.abs(x)))


def _mod_kernel(c_ref, w_ref, b_ref, o_ref):
    a = _silu(c_ref[...]).astype(BF16)
    o_ref[...] = jnp.dot(a, w_ref[...].astype(BF16), preferred_element_type=F32) + b_ref[...]


def _ada_mod(cond, w_mod, b_mod):
    L, D, N = w_mod.shape
    tn = _pick(N, (512, 256, 128))
    vm = 2 * (D * tn * 4 + 16 * tn * 4 * 2) + 16 * D * 4 * 2 + D * tn * 2
    return pl.pallas_call(
        _mod_kernel,
        grid=(L, N // tn),
        in_specs=[
            pl.BlockSpec((16, D), lambda l, j: (0, 0)),
            pl.BlockSpec((None, D, tn), lambda l, j: (l, 0, j)),
            pl.BlockSpec((None, 1, tn), lambda l, j: (l, 0, j)),
        ],
        out_specs=pl.BlockSpec((None, 16, tn), lambda l, j: (l, 0, j)),
        out_shape=jax.ShapeDtypeStruct((L, 16, N), F32),
        compiler_params=_cparams(("parallel", "parallel"), vm),
        name="ada_mod",
    )(cond, w_mod, b_mod.reshape(L, 1, N))


def _norm_mod_kernel(x_ref, g_ref, sh_ref, sc_ref, o_ref):
    x = x_ref[...]
    y = x * lax.rsqrt(jnp.mean(x * x, axis=-1, keepdims=True) + NORM_EPS) * g_ref[...]
    o_ref[...] = (y * (1.0 + sc_ref[...]) + sh_ref[...]).astype(o_ref.dtype)


def _norm_mod(x, g, shift2, scale2, n_lat, rows):
    D = x.shape[1]
    tr = ROW_TILE
    nl = n_lat // tr
    vm = 2 * (tr * D * 4 + tr * D * 2) + 6 * D * 4
    return pl.pallas_call(
        _norm_mod_kernel,
        grid=(rows // tr,),
        in_specs=[
            pl.BlockSpec((tr, D), lambda i: (i, 0)),
            pl.BlockSpec((1, D), lambda i: (0, 0)),
            pl.BlockSpec((None, 1, D), lambda i: (jnp.where(i >= nl, 1, 0), 0, 0)),
            pl.BlockSpec((None, 1, D), lambda i: (jnp.where(i >= nl, 1, 0), 0, 0)),
        ],
        out_specs=pl.BlockSpec((tr, D), lambda i: (i, 0)),
        out_shape=jax.ShapeDtypeStruct((rows, D), BF16),
        compiler_params=_cparams(("parallel",), vm),
        name="norm_mod",
    )(x, g.reshape(1, D), shift2, scale2)


def _final_norm_kernel(x_ref, g_ref, o_ref):
    x = x_ref[...]
    o_ref[...] = x * lax.rsqrt(jnp.mean(x * x, axis=-1, keepdims=True) + NORM_EPS) * g_ref[...]


def _final_norm(x, g, rows):
    D = x.shape[1]
    tr = ROW_TILE
    return pl.pallas_call(
        _final_norm_kernel,
        grid=(rows // tr,),
        in_specs=[pl.BlockSpec((tr, D), lambda i: (i, 0)), pl.BlockSpec((1, D), lambda i: (0, 0))],
        out_specs=pl.BlockSpec((tr, D), lambda i: (i, 0)),
        out_shape=jax.ShapeDtypeStruct((rows, D), F32),
        compiler_params=_cparams(("parallel",), 4 * tr * D * 4),
        name="final_norm",
    )(x, g.reshape(1, D))


def _mm_kernel(*refs, nk, epi, tm, n_lat, transposed):
    if nk > 1:
        acc_ref = refs[-1]
        refs = refs[:-1]
    a_ref, w_ref = refs[0], refs[1]
    o_ref = refs[-1]
    extra = refs[2:-1]
    row0 = pl.program_id(0) * tm

    def finish(acc):
        if epi == "cast":
            o_ref[...] = acc.astype(o_ref.dtype)
        elif epi == "sigmoid_bias":
            o_ref[...] = _sigmoid(acc + extra[0][...]).astype(o_ref.dtype)
        elif epi == "residual":
            res_ref, gate_ref = extra
            row = row0 + lax.broadcasted_iota(jnp.int32, acc.shape, 0)
            gate = jnp.where(row < n_lat, gate_ref[0:1, :], gate_ref[1:2, :])
            o_ref[...] = res_ref[...] + gate * acc
        else:
            raise ValueError(epi)

    w = w_ref[...].astype(BF16)
    if transposed:
        part = lax.dot_general(a_ref[...], w, (((1,), (1,)), ((), ())), preferred_element_type=F32)
    else:
        part = jnp.dot(a_ref[...], w, preferred_element_type=F32)
    if nk == 1:
        finish(part)
        return
    k = pl.program_id(2)

    @pl.when(k == 0)
    def _():
        acc_ref[...] = part

    @pl.when(k > 0)
    def _():
        acc_ref[...] += part

    @pl.when(k == nk - 1)
    def _():
        finish(acc_ref[...])


def _mm_tiles(rows, n, col0, K, osz, wsz, residual, any_col0):
    best = None
    for tm in (1408, 1024, 768, 512, 256, 128, 64):
        if rows % tm:
            continue
        for tn in (2048, 1024, 512, 256, 128):
            if n % tn or (col0 % tn and not any_col0):
                continue
            for tk in ((K,) if K <= MAX_FULL_K else range(LANES, MAX_TK + 1, LANES)):
                if K % tk:
                    continue
                vm = 2 * (tm * tk * 2 + tk * tn * wsz + tm * tn * osz) + tm * tn * 4
                vm += tk * tn * 2 if wsz > 2 else 0
                vm += tm * tn * 4 if tk < K else 0
                vm += 2 * tm * tn * 4 if residual else 0
                if vm > MM_VMEM_BUDGET:
                    continue
                intensity = tm * tn / (tm + tn)
                key = (min(intensity, MM_INTENSITY_ENOUGH), tk, intensity)
                if best is None or key > best[0]:
                    best = (key, (tm, tn, tk), vm)
    assert best is not None
    return best[1], best[2]


def _matmul(a, w, layer, *, rows, out_dtype, col0=0, n=None, transposed=False, epi="cast", bias=None, res=None,
            gate2=None, n_lat=0, name="mm"):
    K = w.shape[2] if transposed else w.shape[1]
    n = (w.shape[1] if transposed else w.shape[2]) - col0 if n is None else n
    (tm, tn, tk), vm = _mm_tiles(rows, n, col0, K, jnp.dtype(out_dtype).itemsize, w.dtype.itemsize,
                                 epi == "residual", transposed)
    nk = K // tk
    if transposed:
        assert nk == 1 and col0 % SUBLANES_F32 == 0
        w_spec = pl.BlockSpec((None, pl.Element(tn), pl.Element(K)),
                              lambda i, j, k: (layer, pl.multiple_of(col0 + j * tn, SUBLANES_F32), 0))
    else:
        cb = col0 // tn
        w_spec = pl.BlockSpec((None, tk, tn), lambda i, j, k: (layer, k, cb + j))
    in_specs = [pl.BlockSpec((tm, tk), lambda i, j, k: (i, k)), w_spec]
    args = [a, w]
    if epi == "sigmoid_bias":
        in_specs.append(pl.BlockSpec((1, tn), lambda i, j, k: (0, j)))
        args.append(bias.reshape(1, n))
    elif epi == "residual":
        in_specs.append(pl.BlockSpec((tm, tn), lambda i, j, k: (i, j)))
        in_specs.append(pl.BlockSpec((2, tn), lambda i, j, k: (0, j)))
        args += [res, gate2]
    scratch = [pltpu.VMEM((tm, tn), F32)] if nk > 1 else []
    return pl.pallas_call(
        functools.partial(_mm_kernel, nk=nk, epi=epi, tm=tm, n_lat=n_lat, transposed=transposed),
        grid=(rows // tm, n // tn, nk),
        in_specs=in_specs,
        out_specs=pl.BlockSpec((tm, tn), lambda i, j, k: (i, j)),
        out_shape=jax.ShapeDtypeStruct((rows, n), out_dtype),
        scratch_shapes=scratch,
        compiler_params=_cparams(("parallel", "parallel", "arbitrary"), vm),
        name=name,
    )(*args)


def _qk_prep_kernel(zq_ref, zk_ref, c_ref, sa_ref, sb_ref, gq_ref, gk_ref, q_ref, k_ref, *, hd):
    cos, sa, sb = c_ref[...], sa_ref[...], sb_ref[...]

    def norm_rope(x, g):
        y = x * lax.rsqrt(jnp.mean(x * x, axis=-1, keepdims=True) + NORM_EPS) * g
        return y * cos + pltpu.roll(y, hd - hd // 4, 1) * sa + pltpu.roll(y, hd // 4, 1) * sb

    gq = gq_ref[...] * (hd ** -0.5 * LOG2_E)
    for h in range(zq_ref.shape[1] // hd):
        sl = slice(h * hd, (h + 1) * hd)
        q_ref[:, sl] = norm_rope(zq_ref[:, sl].astype(F32), gq).astype(BF16)
    gk = gk_ref[...]
    for h in range(zk_ref.shape[1] // hd):
        sl = slice(h * hd, (h + 1) * hd)
        k_ref[:, sl] = norm_rope(zk_ref[:, sl].astype(F32), gk).astype(BF16)


def _qk_prep(z1, tabs, gq, gk, aw, kvw, hd):
    R = z1.shape[0]
    tr = ROW_TILE
    assert aw % kvw == 0
    cos, sa, sb = tabs
    tab_spec = pl.BlockSpec((tr, hd), lambda i: (i, 0))
    vm = 2 * 2 * (tr * aw * 2 + tr * kvw * 2) + 6 * tr * hd * 4 + 8 * tr * hd * 4
    return pl.pallas_call(
        functools.partial(_qk_prep_kernel, hd=hd),
        grid=(R // tr,),
        in_specs=[
            pl.BlockSpec((tr, aw), lambda i: (i, 0)),
            pl.BlockSpec((tr, kvw), lambda i: (i, aw // kvw)),
            tab_spec, tab_spec, tab_spec,
            pl.BlockSpec((1, hd), lambda i: (0, 0)),
            pl.BlockSpec((1, hd), lambda i: (0, 0)),
        ],
        out_specs=[pl.BlockSpec((tr, aw), lambda i: (i, 0)), pl.BlockSpec((tr, kvw), lambda i: (i, 0))],
        out_shape=[jax.ShapeDtypeStruct((R, aw), BF16), jax.ShapeDtypeStruct((R, kvw), BF16)],
        compiler_params=_cparams(("parallel",), vm),
        name="qk_prep",
    )(z1, z1, cos, sa, sb, gq.reshape(1, hd), gk.reshape(1, hd))


def _attn_kernel(q_ref, k_ref, v_ref, o_ref, *, hd, tk, nkv):
    tq = q_ref.shape[0]
    q4 = jnp.concatenate([q_ref[:, g * hd:(g + 1) * hd] for g in range(Q_PER_KV)], axis=0)

    ones_col = jnp.where(lax.broadcasted_iota(jnp.int32, (tk, hd), 1) == 0, 1.0, 0.0).astype(v_ref.dtype)
    m = acc = None
    for j in range(nkv):
        kc = k_ref[j * tk:(j + 1) * tk, :]
        ve = jnp.concatenate([v_ref[j * tk:(j + 1) * tk, :], ones_col], axis=1)
        s = lax.dot_general(q4, kc, (((1,), (1,)), ((), ())), preferred_element_type=F32)
        smax = jnp.max(s, axis=-1, keepdims=True)
        if j == 0:
            m = smax
            acc = jnp.dot(jnp.exp2(s - m).astype(BF16), ve, preferred_element_type=F32)
        else:
            m_new = jnp.maximum(m, smax)
            alpha = jnp.exp2(m - m_new)
            acc = alpha * acc + jnp.dot(jnp.exp2(s - m_new).astype(BF16), ve, preferred_element_type=F32)
            m = m_new
    o = acc[:, :hd] / acc[:, hd:hd + 1]
    for g in range(Q_PER_KV):
        o_ref[:, g * hd:(g + 1) * hd] = o[g * tq:(g + 1) * tq].astype(o_ref.dtype)


def _attention(q, k, z1, *, q_row0, q_rows, kv_row0, kv_rows, v_col0, hd):
    aw, kvw = q.shape[1], k.shape[1]
    n_kv = kvw // hd
    gw = Q_PER_KV * hd
    tq = _pick(q_rows, (256, 128))
    tk = _pick(kv_rows, (768, 512, 256, 128))
    assert q_row0 % tq == 0 and kv_row0 % kv_rows == 0 and v_col0 % hd == 0
    qb0, kb0, vc0 = q_row0 // tq, kv_row0 // kv_rows, v_col0 // hd
    vm = 2 * (2 * tq * gw * 2 + 2 * kv_rows * hd * 2) + 6 * Q_PER_KV * tq * tk * 4
    return pl.pallas_call(
        functools.partial(_attn_kernel, hd=hd, tk=tk, nkv=kv_rows // tk),
        grid=(n_kv, q_rows // tq),
        in_specs=[
            pl.BlockSpec((tq, gw), lambda h, i: (qb0 + i, h)),
            pl.BlockSpec((kv_rows, hd), lambda h, i: (kb0, h)),
            pl.BlockSpec((kv_rows, hd), lambda h, i: (kb0, vc0 + h)),
        ],
        out_specs=pl.BlockSpec((tq, gw), lambda h, i: (i, h)),
        out_shape=jax.ShapeDtypeStruct((q_rows, aw), BF16),
        compiler_params=_cparams(("parallel", "parallel"), vm),
        name="attention",
    )(q, k, z1)


def _halo_specs(tr, hr, width, col_block, n_rows):
    per = tr // hr
    last = n_rows // hr - 1
    prev = pl.BlockSpec((hr, width), lambda i, *_: (jnp.maximum(i * per - 1, 0), col_block(*_)))
    nxt = pl.BlockSpec((hr, width), lambda i, *_: (jnp.minimum((i + 1) * per, last), col_block(*_)))
    return prev, nxt


def _edge_flags(i, tr, n_lat, n_rows):
    r0 = i * tr
    r1 = r0 + tr
    pm = jnp.where((r0 == 0) | (r0 == n_lat), 0.0, 1.0).astype(F32)
    nm = jnp.where((r1 == n_lat) | (r1 == n_rows), 0.0, 1.0).astype(F32)
    return pm, nm


def _dwconv3(x, prev_row, next_row, w):
    tr = x.shape[0]
    rid = lax.broadcasted_iota(jnp.int32, x.shape, 0)
    xp = jnp.where(rid == 0, prev_row, pltpu.roll(x, 1, 0))
    xn = jnp.where(rid == tr - 1, next_row, pltpu.roll(x, tr - 1, 0))
    return xp * w[0:1] + x * w[1:2] + xn * w[2:3]


def _conv_branch_kernel(h_ref, hp_ref, hn_ref, b_ref, c_ref, cp_ref, cn_ref, w_ref, o_ref, *, n_lat, n_rows):
    tr = h_ref.shape[0]
    pm, nm = _edge_flags(pl.program_id(0), tr, n_lat, n_rows)
    hr = hp_ref.shape[0]
    u = c_ref[...].astype(F32) * h_ref[...].astype(F32)
    up = cp_ref[hr - 1:hr, :].astype(F32) * hp_ref[hr - 1:hr, :].astype(F32) * pm
    un = cn_ref[0:1, :].astype(F32) * hn_ref[0:1, :].astype(F32) * nm
    o_ref[...] = (b_ref[...].astype(F32) * _dwconv3(u, up, un, w_ref[...])).astype(o_ref.dtype)


def _conv_branch(z1, conv_w, *, rows, col0, cw, n_lat):
    tr, hr = ROW_TILE, SUBLANES_BF16
    assert col0 % cw == 0
    cb = col0 // cw
    main = lambda o: pl.BlockSpec((tr, cw), lambda i: (i, cb + o))
    hp, hn = _halo_specs(tr, hr, cw, lambda: cb, rows)
    cp, cn = _halo_specs(tr, hr, cw, lambda: cb + 2, rows)
    vm = 2 * 4 * tr * cw * 2 + 8 * tr * cw * 4
    return pl.pallas_call(
        functools.partial(_conv_branch_kernel, n_lat=n_lat, n_rows=rows),
        grid=(rows // tr,),
        in_specs=[main(0), hp, hn, main(1), main(2), cp, cn, pl.BlockSpec((3, cw), lambda i: (0, 0))],
        out_specs=pl.BlockSpec((tr, cw), lambda i: (i, 0)),
        out_shape=jax.ShapeDtypeStruct((rows, cw), BF16),
        compiler_params=_cparams(("parallel",), vm),
        name="conv_branch",
    )(z1, z1, z1, z1, z1, z1, z1, conv_w)


def _ffn_act_kernel(v_ref, vp_ref, vn_ref, g_ref, gp_ref, gn_ref, wv_ref, wg_ref, bv_ref, bg_ref, o_ref, *,
                    n_lat, n_rows):
    tr = v_ref.shape[0]
    pm, nm = _edge_flags(pl.program_id(0), tr, n_lat, n_rows)
    hr = vp_ref.shape[0]
    ri = lax.broadcasted_iota(jnp.int32, (2 * tr, tr), 0)
    ci = lax.broadcasted_iota(jnp.int32, (2 * tr, tr), 1)
    shift = jnp.where(ci == jnp.where(ri < tr, ri - 1, ri - tr + 1), 1.0, 0.0).astype(v_ref.dtype)
    r8 = lax.broadcasted_iota(jnp.int32, (SUBLANES_F32, v_ref.shape[1]), 0)

    def conv(x_ref, p_ref, n_ref, w_ref, b_ref):
        x = x_ref[...]
        sh = jnp.dot(shift, x, preferred_element_type=F32)
        first = sh[:SUBLANES_F32] + jnp.where(r8 == 0, p_ref[hr - 1:hr, :].astype(F32) * pm, 0.0)
        lastr = sh[2 * tr - SUBLANES_F32:] + jnp.where(r8 == SUBLANES_F32 - 1, n_ref[0:1, :].astype(F32) * nm, 0.0)
        xp = jnp.concatenate([first, sh[SUBLANES_F32:tr]], axis=0)
        xn = jnp.concatenate([sh[tr:2 * tr - SUBLANES_F32], lastr], axis=0)
        w = w_ref[...]
        return xp * w[0:1] + x.astype(F32) * w[1:2] + xn * w[2:3] + b_ref[...]

    val = conv(v_ref, vp_ref, vn_ref, wv_ref, bv_ref)
    gate = conv(g_ref, gp_ref, gn_ref, wg_ref, bg_ref)
    tc = v_ref.shape[1]
    o_ref[:, :tc] = (_silu(gate) * val).astype(o_ref.dtype)
    if o_ref.shape[1] > tc:
        o_ref[:, tc:] = jnp.zeros((tr, o_ref.shape[1] - tc), o_ref.dtype)


def _ffn_blocks(ff):
    tc = max(t for t in range(LANES, MAX_FFN_COLS + 1, LANES) if ff % t == 0)
    nb = ff // tc
    pad = next(p for p in range(0, FFN_K_UNIT + 1, LANES) if (nb * (tc + p)) % FFN_K_UNIT == 0)
    return tc, nb, pad


def _ffn_act(u, conv_w, conv_b, *, rows, n_lat):
    ff = u.shape[1] // 2
    tr, hr = ROW_TILE // 2, SUBLANES_BF16
    tc, nb, pad = _ffn_blocks(ff)
    vp, vn = _halo_specs(tr, hr, tc, lambda j: j, rows)
    gp, gn = _halo_specs(tr, hr, tc, lambda j: nb + j, rows)
    wspec = lambda o: pl.BlockSpec((3, tc), lambda i, j: (0, o * nb + j))
    bspec = lambda o: pl.BlockSpec((1, tc), lambda i, j: (0, o * nb + j))
    vm = 2 * 3 * tr * tc * 2 + 12 * tr * tc * 4
    return pl.pallas_call(
        functools.partial(_ffn_act_kernel, n_lat=n_lat, n_rows=rows),
        grid=(rows // tr, nb),
        in_specs=[
            pl.BlockSpec((tr, tc), lambda i, j: (i, j)), vp, vn,
            pl.BlockSpec((tr, tc), lambda i, j: (i, nb + j)), gp, gn,
            wspec(0), wspec(1), bspec(0), bspec(1),
        ],
        out_specs=pl.BlockSpec((tr, tc + pad), lambda i, j: (i, j)),
        out_shape=jax.ShapeDtypeStruct((rows, nb * (tc + pad)), BF16),
        compiler_params=_cparams(("parallel", "parallel"), vm),
        name="ffn_act",
    )(u, u, u, u, u, u, conv_w, conv_w, conv_b, conv_b)


def _head_sums(x, hs):
    assert hs & (hs - 1) == 0
    li = lax.broadcasted_iota(jnp.int32, (LANES, LANES), 0) & -hs
    lj = lax.broadcasted_iota(jnp.int32, (LANES, LANES), 1) & -hs
    seg = jnp.where(li == lj, 1.0, 0.0).astype(F32)
    parts = [jnp.dot(x[:, t * LANES:(t + 1) * LANES], seg, preferred_element_type=F32,
                     precision=lax.Precision.HIGHEST) for t in range(x.shape[1] // LANES)]
    return jnp.concatenate(parts, axis=1)


def _rwkv_feat_kernel(z_ref, zp_ref, zn_ref, lo_ref, cw_ref, w2_ref, w0_ref, a2_ref, a0_ref, g2_ref, kk_ref,
                      ka_ref, r_o, v_o, kn_o, ld0_o, ld1_o, kd0_o, kd1_o, b0_o, b1_o, g_o, *,
                      n_lat, n_rows, dl2, il2, hs):
    tr, w3 = z_ref.shape
    W = w3 // 3
    pm, nm = _edge_flags(pl.program_id(0), tr, n_lat, n_rows)
    hr = zp_ref.shape[0]
    rkv = _dwconv3(z_ref[...], zp_ref[hr - 1:hr, :] * pm, zn_ref[0:1, :] * nm, cw_ref[...])
    r, k, v = rkv[:, :W], rkv[:, W:2 * W], rkv[:, 2 * W:]
    lw = lo_ref[:, :dl2]
    la = lo_ref[:, dl2:dl2 + il2]
    lg = lo_ref[:, dl2 + il2:]
    wl = jnp.dot(jnp.tanh(lw).astype(BF16), w2_ref[...], preferred_element_type=F32) + w0_ref[...]
    ld = -jnp.exp(-_softplus(-wl) - 0.5)
    a = _sigmoid(jnp.dot(la.astype(BF16), a2_ref[...], preferred_element_type=F32) + a0_ref[...])
    g = jnp.dot(_sigmoid(lg).astype(BF16), g2_ref[...], preferred_element_type=F32)
    kk = k * kk_ref[...]
    kn = kk / jnp.maximum(jnp.sqrt(_head_sums(kk * kk, hs)), 1e-12)
    ka = ka_ref[...]
    r_o[...] = r
    v_o[...] = v
    kn_o[...] = kn
    g_o[...] = g
    for d, (ld_o, kd_o, b_o) in enumerate(((ld0_o, kd0_o, b0_o), (ld1_o, kd1_o, b1_o))):
        ad = a[:, d * W:(d + 1) * W]
        ld_o[...] = ld[:, d * W:(d + 1) * W]
        kd_o[...] = k * (1.0 + (ad - 1.0) * ka)
        b_o[...] = kn * ad


def _rwkv_features(z2, lo, p, *, n_lat, hs):
    R, w3 = z2.shape
    W = w3 // 3
    lwp = lo.shape[1]
    tr, hr = ROW_TILE // 2, SUBLANES_F32
    dl2, il2 = p["w2"].shape[0], p["a2"].shape[0]
    zp, zn = _halo_specs(tr, hr, w3, lambda: 0, R)
    full = lambda a: pl.BlockSpec(a.shape, lambda i: (0,) * a.ndim)
    out_spec = pl.BlockSpec((tr, W), lambda i: (i, 0))
    vm = 2 * (tr * w3 * 4 + tr * lwp * 4 + 10 * tr * W * 4) + 12 * tr * w3 * 4
    outs = pl.pallas_call(
        functools.partial(_rwkv_feat_kernel, n_lat=n_lat, n_rows=R, dl2=dl2, il2=il2, hs=hs),
        grid=(R // tr,),
        in_specs=[pl.BlockSpec((tr, w3), lambda i: (i, 0)), zp, zn, pl.BlockSpec((tr, lwp), lambda i: (i, 0)),
                  full(p["cw"]), full(p["w2"]), full(p["w0"]), full(p["a2"]), full(p["a0"]), full(p["g2"]),
                  full(p["k_k"]), full(p["k_a"])],
        out_specs=[out_spec] * 10,
        out_shape=[jax.ShapeDtypeStruct((R, W), F32)] * 10,
        compiler_params=_cparams(("parallel",), vm),
        name="rwkv_features",
    )(z2, z2, z2, lo, p["cw"], p["w2"], p["w0"], p["a2"], p["a0"], p["g2"], p["k_k"], p["k_a"])
    return dict(zip(("r", "v", "kn", "ld0", "ld1", "kd0", "kd1", "b0", "b1", "g"), outs))


def _scan_kernel(rf, vf, kf, ldf, kdf, bf, rb, vb, kb, ldb, kdb, bb, yf_o, yb_o, st_ref):
    @pl.when(pl.program_id(0) == 0)
    def _():
        st_ref[...] = jnp.zeros_like(st_ref)

    C, W = rf.shape
    C2 = 2 * C
    dot = functools.partial(jnp.dot, preferred_element_type=F32)
    dot_nt = lambda a, b: lax.dot_general(a, b, (((1,), (1,)), ((), ())), preferred_element_type=F32)
    ri = lax.broadcasted_iota(jnp.int32, (C, C), 0)
    ci = lax.broadcasted_iota(jnp.int32, (C, C), 1)
    ii = lax.broadcasted_iota(jnp.int32, (C2, C2), 0)
    jj = lax.broadcasted_iota(jnp.int32, (C2, C2), 1)
    im, jm = ii & (C - 1), jj & (C - 1)
    eye = jnp.where(ii == jj, 1.0, 0.0).astype(F32)
    same_head = (ii & -C) == (jj & -C)
    h0 = lax.broadcasted_iota(jnp.int32, (C, LANES), 1) < LANES // 2

    def stack(x):
        return jnp.concatenate([jnp.where(h0, x, 0.0), jnp.where(h0, 0.0, x)], axis=0)

    def fold(x):
        return x[:C] + x[C:]

    inst = []
    for d, (r_, v_, k_, ld_, kd_, b_, y_o) in enumerate(((rf, vf, kf, ldf, kdf, bf, yf_o),
                                                          (rb, vb, kb, ldb, kdb, bb, yb_o))):
        rev = d == 1
        ld, kd, b = ld_[...], kd_[...], b_[...]
        inc = jnp.where((ci >= ri) if rev else (ci <= ri), 1.0, 0.0).astype(F32)
        cum = jnp.dot(inc, ld, preferred_element_type=F32, precision=lax.Precision.HIGHEST)
        tot = cum[0:1, :] if rev else cum[C - 1:C, :]
        e_neg = jnp.exp(-cum)
        e_dec = jnp.exp(tot - cum)
        full = dict(kt=k_[...] * jnp.exp(cum - ld), rt=r_[...] * jnp.exp(cum), kh=kd * e_neg, bh=b * e_neg,
                    kdec=kd * e_dec, bdec=b * e_dec, v=v_[...], etot=jnp.exp(tot))
        strict = (jm > im) if rev else (jm < im)
        incl = (jm >= im) if rev else (jm <= im)
        for p in range(W // LANES):
            q = {n: a[:, p * LANES:(p + 1) * LANES] for n, a in full.items()}
            q.update(d=d, p=p, rev=rev, strict=strict, incl=incl, y_o=y_o)
            inst.append(q)

    for q in inst:
        lhs = jnp.concatenate([stack(q["kt"]), stack(q["rt"])], axis=0).astype(BF16)
        rhs = jnp.concatenate([stack(q["bh"]), stack(q["kh"])], axis=0).astype(BF16)
        q["G"] = dot_nt(lhs, rhs)
    for q in inst:
        G = q.pop("G")
        q["Lb"] = jnp.where(q["strict"], G[:C2, :C2], 0.0)
        q["LkAk"] = jnp.concatenate([jnp.where(q["strict"], G[:C2, C2:], 0.0),
                                     jnp.where(q["incl"], G[C2:, C2:], 0.0)], axis=0).astype(BF16)
        q["Ab"] = jnp.where(q["incl"], G[C2:, :C2], 0.0).astype(BF16)

    s = 1
    while s < C:
        same = (ii & -(2 * s)) == (jj & -(2 * s))
        hi_i, hi_j = (ii & s) != 0, (jj & s) != 0
        off = {False: same & hi_i & ~hi_j, True: same & ~hi_i & hi_j}
        if s == 1:
            for q in inst:
                q["T"] = eye - jnp.where(off[q["rev"]], q["Lb"], 0.0)
        else:
            for q in inst:
                q["Tb"] = q["T"].astype(BF16)
                q["TM"] = dot(q["Tb"], jnp.where(off[q["rev"]], q["Lb"], 0.0).astype(BF16))
            for q in inst:
                q["T"] = q["T"] - dot(q.pop("TM").astype(BF16), q.pop("Tb"))
        s *= 2

    for q in inst:
        S = st_ref[q["d"], q["p"]]
        q["S"] = S
        q["X"] = dot_nt(jnp.concatenate([q["kt"], q["rt"]], axis=0).astype(BF16), S.astype(BF16))
        q["LA"] = dot(q.pop("LkAk"), stack(q["v"]).astype(BF16))
    for q in inst:
        q["Us"] = dot(q.pop("T").astype(BF16), (stack(q["X"][:C]) + q["LA"][:C2]).astype(BF16))
    for q in inst:
        q["ABU"] = dot(q.pop("Ab"), q["Us"].astype(BF16))
    for q in inst:
        sl = slice(q["p"] * LANES, (q["p"] + 1) * LANES)
        q["y_o"][:, sl] = q["X"][C:] + fold(q["LA"][C2:] - q.pop("ABU"))
        vu_t = jnp.concatenate([q["v"], -fold(q["Us"])], axis=0).T.astype(BF16)
        upd = dot(vu_t, jnp.concatenate([q["kdec"], q["bdec"]], axis=0).astype(BF16))
        st_ref[q["d"], q["p"]] = q["S"] * q["etot"] + jnp.where(same_head, upd, 0.0)


def _rwkv_scan(f, *, n_lat):
    R, W = f["r"].shape
    C = CHUNK
    n, nl = R // C, n_lat // C
    nc = n - nl
    fwd = lambda s: (jnp.where(s < nc, nl + s, s - nc), 0)
    bwd = lambda s: (n - 1 - s, 0)
    fs, bs = pl.BlockSpec((C, W), fwd), pl.BlockSpec((C, W), bwd)
    inst_tmp = 24 * LANES * LANES * 4
    vm = 2 * 14 * C * W * 4 + 2 * (W // LANES) * (LANES * LANES * 4 + inst_tmp)
    return pl.pallas_call(
        _scan_kernel,
        grid=(n,),
        in_specs=[fs] * 6 + [bs] * 6,
        out_specs=[fs, bs],
        out_shape=[jax.ShapeDtypeStruct((R, W), F32)] * 2,
        scratch_shapes=[pltpu.VMEM((2, W // LANES, LANES, LANES), F32)],
        compiler_params=_cparams(("arbitrary",), vm),
        name="rwkv_scan",
    )(f["r"], f["v"], f["kn"], f["ld0"], f["kd0"], f["b0"], f["r"], f["v"], f["kn"], f["ld1"], f["kd1"], f["b1"])


def _rwkv_readout_kernel(yf_ref, yb_ref, r_ref, v_ref, kd0_ref, kd1_ref, g_ref, rk_ref, lw_ref, lb_ref, o_ref, *,
                         hs):
    y = yf_ref[...] + yb_ref[...]
    mu = _head_sums(y, hs) * (1.0 / hs)
    d = y - mu
    var = _head_sums(d * d, hs) * (1.0 / hs)
    yn = d * lax.rsqrt(var + GN_EPS) * lw_ref[...] + lb_ref[...]
    bonus = _head_sums(r_ref[...] * (kd0_ref[...] + kd1_ref[...]) * rk_ref[...], hs) * v_ref[...]
    o_ref[...] = ((yn + bonus) * g_ref[...]).astype(o_ref.dtype)


def _rwkv_readout(yf, yb, f, p, *, rows, hs):
    W = yf.shape[1]
    tr = ROW_TILE
    rs = pl.BlockSpec((tr, W), lambda i: (i, 0))
    ps = pl.BlockSpec((1, W), lambda i: (0, 0))
    vm = 2 * 8 * tr * W * 4 + 10 * tr * W * 4
    return pl.pallas_call(
        functools.partial(_rwkv_readout_kernel, hs=hs),
        grid=(rows // tr,),
        in_specs=[rs] * 7 + [ps] * 3,
        out_specs=rs,
        out_shape=jax.ShapeDtypeStruct((rows, W), BF16),
        compiler_params=_cparams(("parallel",), vm),
        name="rwkv_readout",
    )(yf, yb, f["r"], f["v"], f["kd0"], f["kd1"], f["g"], p["r_k"], p["ln_w"], p["ln_b"])


def _merge_kernel(a_ref, c_ref, r_ref, wa_ref, wc_ref, wr_ref, ga_ref, gc_ref, gr_ref, o_ref):
    ya = jnp.dot(a_ref[...], wa_ref[...], preferred_element_type=F32)
    yc = jnp.dot(c_ref[...], wc_ref[...], preferred_element_type=F32)
    yr = jnp.dot(r_ref[...], wr_ref[...], preferred_element_type=F32)
    o_ref[...] = (ga_ref[...].astype(F32) * ya + gc_ref[...].astype(F32) * yc
                  + gr_ref[...].astype(F32) * yr).astype(o_ref.dtype)


def _merge(att, cv, rw, wa, wc, wr, layer, gates, *, rows):
    D = wa.shape[2]
    tm = _pick(rows, (512, 256, 128, 64))
    tn = _pick(D, (1024, 512, 256, 128))
    nb = D // tn
    ka, kc, kr = wa.shape[1], wc.shape[1], wr.shape[1]
    vm = 2 * ((tm + tn) * (ka + kc + kr) * 2 + 4 * tm * tn * 2) + 4 * tm * tn * 4
    return pl.pallas_call(
        _merge_kernel,
        grid=(nb, rows // tm),
        in_specs=[
            pl.BlockSpec((tm, ka), lambda j, i: (i, 0)),
            pl.BlockSpec((tm, kc), lambda j, i: (i, 0)),
            pl.BlockSpec((tm, kr), lambda j, i: (i, 0)),
            pl.BlockSpec((None, ka, tn), lambda j, i: (layer, 0, j)),
            pl.BlockSpec((None, kc, tn), lambda j, i: (layer, 0, j)),
            pl.BlockSpec((None, kr, tn), lambda j, i: (layer, 0, j)),
            pl.BlockSpec((tm, tn), lambda j, i: (i, j)),
            pl.BlockSpec((tm, tn), lambda j, i: (i, nb + j)),
            pl.BlockSpec((tm, tn), lambda j, i: (i, 2 * nb + j)),
        ],
        out_specs=pl.BlockSpec((tm, tn), lambda j, i: (i, j)),
        out_shape=jax.ShapeDtypeStruct((rows, D), BF16),
        compiler_params=_cparams(("parallel", "parallel"), vm),
        name="merge",
    )(att, cv, rw, wa, wc, wr, gates, gates, gates)


def _rope_tables(n_lat, n_ctx, hd):
    rows = n_lat // GRID_W
    row = jnp.repeat(jnp.arange(rows, dtype=F32), GRID_W)
    col = jnp.tile(jnp.arange(GRID_W, dtype=F32), rows)
    half = hd // 2
    inv_freq = ROPE_THETA ** (-jnp.arange(0, half, 2, dtype=F32) / half)
    ar, ac = row[:, None] * inv_freq, col[:, None] * inv_freq
    z = jnp.zeros_like(ar)
    cos = jnp.concatenate([jnp.cos(ar), jnp.cos(ar), jnp.cos(ac), jnp.cos(ac)], axis=-1)
    sa = jnp.concatenate([-jnp.sin(ar), z, -jnp.sin(ac), z], axis=-1)
    sb = jnp.concatenate([z, jnp.sin(ar), z, jnp.sin(ac)], axis=-1)
    pad = lambda t, v: jnp.concatenate([t, jnp.full((n_ctx, hd), v, F32)], axis=0)
    return pad(cos, 1.0), pad(sa, 0.0), pad(sb, 0.0)


def _block_diag2(w):
    z = jnp.zeros_like(w[0])
    return jnp.concatenate([jnp.concatenate([w[0], z], axis=1), jnp.concatenate([z, w[1]], axis=1)], axis=0)


def kernel(x, c, ctx, c_ctx, w_mod, b_mod, norm1_g, norm2_g, w_in, b_gate, q_norm_g, k_norm_g, w_attn_o, conv_w, w_conv_o, rwkv_conv_w, rwkv_w0, rwkv_w2, rwkv_a0, rwkv_a2, rwkv_g2, rwkv_k_k, rwkv_k_a, rwkv_r_k, rwkv_ln_w, rwkv_ln_b, w_rwkv_o, w_out, ffn_w_up, ffn_conv_w, ffn_conv_b, ffn_w_down, final_g):
    B, S, D = x.shape
    assert B == 1, "batch is folded away; the problem fixes BATCH = 1"
    Tc = ctx.shape[1]
    R = S + Tc
    L = w_mod.shape[0]
    hd = q_norm_g.shape[1]
    aw = w_attn_o.shape[1]
    cw = conv_w.shape[2]
    W = rwkv_w0.shape[2]
    hs = rwkv_r_k.shape[2]
    dl, il, gl = rwkv_w2.shape[2], rwkv_a2.shape[2], rwkv_g2.shape[1]
    ff = ffn_w_down.shape[1]
    kvw = (w_in.shape[2] - (aw + 3 * cw + 3 * W + 2 * dl + 2 * il + gl + 3 * D)) // 2
    assert S % ROW_TILE == 0 and Tc % ROW_TILE == 0 and S % GRID_W == 0
    assert hd == LANES and 2 * hs == LANES and 2 * CHUNK == LANES and W % LANES == 0

    o_conv = aw + 2 * kvw
    o_rkv = o_conv + 3 * cw
    o_lora = o_rkv + 3 * W
    o_gate = o_lora + 2 * dl + 2 * il + gl
    lw_real = 2 * dl + 2 * il + gl
    lwp = -(-lw_real // LANES) * LANES
    gp = lwp - 2 * dl - 2 * il

    xa = jnp.concatenate([x[0], ctx[0]], axis=0)
    cond = jnp.zeros((16, D), F32).at[0].set(c[0]).at[1].set(c_ctx)
    mods = _ada_mod(cond, w_mod, b_mod)
    tabs = _rope_tables(S, Tc, hd)

    w_t = jnp.swapaxes(w_in, 1, 2)
    w_ao, w_co, w_ro, w_o = (w.astype(BF16) for w in (w_attn_o, w_conv_o, w_rwkv_o, w_out))
    f_tc, f_nb, f_pad = _ffn_blocks(ff)
    w_dn = jnp.pad(ffn_w_down.reshape(L, f_nb, f_tc, D), ((0, 0), (0, 0), (0, f_pad), (0, 0)))
    w_dn = w_dn.reshape(L, f_nb * (f_tc + f_pad), D).astype(BF16)

    for l in range(L):
        last = l == L - 1
        rows = S if last else R
        m = mods[l, :2].reshape(2, N_MOD, 1, D)
        mod = lambda i: m[:, i]
        rp = {
            "cw": rwkv_conv_w[l],
            "w2": _block_diag2(rwkv_w2[l]).astype(BF16),
            "w0": rwkv_w0[l].reshape(1, 2 * W),
            "a2": _block_diag2(rwkv_a2[l]).astype(BF16),
            "a0": rwkv_a0[l].reshape(1, 2 * W),
            "g2": jnp.pad(rwkv_g2[l], ((0, gp - gl), (0, 0))).astype(BF16),
            "k_k": rwkv_k_k[l].reshape(1, W),
            "k_a": rwkv_k_a[l].reshape(1, W),
            "r_k": rwkv_r_k[l].reshape(1, W),
            "ln_w": rwkv_ln_w[l].reshape(1, W),
            "ln_b": rwkv_ln_b[l].reshape(1, W),
        }

        h = _norm_mod(xa, norm1_g[l], mod(0), mod(1), S, R)
        in_proj = functools.partial(_matmul, h, w_t, l, transposed=True)
        z1 = in_proj(rows=R, out_dtype=BF16, col0=0, n=o_rkv, name="in_proj_qkv_conv")
        z2 = in_proj(rows=R, out_dtype=F32, col0=o_rkv, n=3 * W, name="in_proj_rkv")
        lo = in_proj(rows=R, out_dtype=F32, col0=o_lora, n=lwp, name="in_proj_lora")
        gates = in_proj(rows=rows, out_dtype=BF16, col0=o_gate, n=3 * D, epi="sigmoid_bias", bias=b_gate[l],
                        name="in_proj_gates")

        qn, kn = _qk_prep(z1, tabs, q_norm_g[l], k_norm_g[l], aw, kvw, hd)
        att = _attention(qn, kn, z1, q_row0=0, q_rows=S, kv_row0=0, kv_rows=R, v_col0=aw + kvw, hd=hd)
        if not last:
            att_c = _attention(qn, kn, z1, q_row0=S, q_rows=Tc, kv_row0=S, kv_rows=Tc, v_col0=aw + kvw, hd=hd)
            att = jnp.concatenate([att, att_c], axis=0)

        cv = _conv_branch(z1, conv_w[l], rows=rows, col0=o_conv, cw=cw, n_lat=S)

        feat = _rwkv_features(z2, lo, rp, n_lat=S, hs=hs)
        yf, yb = _rwkv_scan(feat, n_lat=S)
        rw = _rwkv_readout(yf, yb, feat, rp, rows=rows, hs=hs)

        mg = _merge(att, cv, rw, w_ao, w_co, w_ro, l, gates, rows=rows)
        xa = _matmul(mg, w_o, l, rows=rows, out_dtype=F32, epi="residual", res=xa, gate2=mod(2).reshape(2, D),
                     n_lat=S, name="out_proj")

        h2 = _norm_mod(xa, norm2_g[l], mod(3), mod(4), S, rows)
        u = _matmul(h2, ffn_w_up, l, rows=rows, out_dtype=BF16, name="ffn_up")
        act = _ffn_act(u, ffn_conv_w[l], ffn_conv_b[l].reshape(1, 2 * ff), rows=rows, n_lat=S)
        xa = _matmul(act, w_dn, l, rows=rows, out_dtype=F32, epi="residual", res=xa, gate2=mod(5).reshape(2, D),
                     n_lat=S, name="ffn_down")

    return _final_norm(xa, final_g, S)[None]
```

```python
import functools

import jax
import jax.numpy as jnp
from jax import lax
from jax.experimental import pallas as pl
from jax.experimental.pallas import tpu as pltpu

F32 = jnp.float32
BF16 = jnp.bfloat16

GRID_W = 64
ROPE_THETA = 10000.0
Q_PER_KV = 4
N_MOD = 6
NORM_EPS = 1e-6
GN_EPS = 64e-5
LOG2_E = 1.4426950408889634

LANES = 128
SUBLANES_F32 = 8
SUBLANES_BF16 = 16
VMEM_LIMIT_CAP = 56 * 1024 * 1024
MM_VMEM_BUDGET = 48 * 1024 * 1024
VMEM_LIMIT_FLOOR = 16 * 1024 * 1024
MM_INTENSITY_ENOUGH = 384
MXU_DIM = 256
MAX_FULL_K = 16 * MXU_DIM
MAX_TK = 22 * MXU_DIM
MAX_FFN_COLS = 24 * MXU_DIM
FFN_K_UNIT = 4 * MXU_DIM
CHUNK = 64
ROW_TILE = 256


def _pick(n, candidates):
    for c in candidates:
        if n % c == 0:
            return c
    raise ValueError(f"no tile in {candidates} divides {n}")


def _cparams(sem, vmem_bytes):
    limit = int(min(max(vmem_bytes * 5 // 4, VMEM_LIMIT_FLOOR), VMEM_LIMIT_CAP))
    return pltpu.CompilerParams(dimension_semantics=sem, vmem_limit_bytes=limit)


def _sigmoid(x):
    return 1.0 / (1.0 + jnp.exp(-x))


def _silu(x):
    return x * _sigmoid(x)


def _softplus(x):
    return jnp.maximum(x, 0.0) + jnp.log(1.0 + jnp.exp(-jnp.abs(x)))


def _mod_kernel(c_ref, w_ref, b_ref, o_ref):
    a = _silu(c_ref[...]).astype(BF16)
    o_ref[...] = jnp.dot(a, w_ref[...].astype(BF16), preferred_element_type=F32) + b_ref[...]


def _ada_mod(cond, w_mod, b_mod):
    L, D, N = w_mod.shape
    tn = _pick(N, (512, 256, 128))
    vm = 2 * (D * tn * 4 + 16 * tn * 4 * 2) + 16 * D * 4 * 2 + D * tn * 2
    return pl.pallas_call(
        _mod_kernel,
        grid=(L, N // tn),
        in_specs=[
            pl.BlockSpec((16, D), lambda l, j: (0, 0)),
            pl.BlockSpec((None, D, tn), lambda l, j: (l, 0, j)),
            pl.BlockSpec((None, 1, tn), lambda l, j: (l, 0, j)),
        ],
        out_specs=pl.BlockSpec((None, 16, tn), lambda l, j: (l, 0, j)),
        out_shape=jax.ShapeDtypeStruct((L, 16, N), F32),
        compiler_params=_cparams(("parallel", "parallel"), vm),
        name="ada_mod",
    )(cond, w_mod, b_mod.reshape(L, 1, N))


def _norm_mod_kernel(x_ref, g_ref, sh_ref, sc_ref, o_ref):
    x = x_ref[...]
    y = x * lax.rsqrt(jnp.mean(x * x, axis=-1, keepdims=True) + NORM_EPS) * g_ref[...]
    o_ref[...] = (y * (1.0 + sc_ref[...]) + sh_ref[...]).astype(o_ref.dtype)


def _norm_mod(x, g, shift2, scale2, n_lat, rows):
    D = x.shape[1]
    tr = ROW_TILE
    nl = n_lat // tr
    vm = 2 * (tr * D * 4 + tr * D * 2) + 6 * D * 4
    return pl.pallas_call(
        _norm_mod_kernel,
        grid=(rows // tr,),
        in_specs=[
            pl.BlockSpec((tr, D), lambda i: (i, 0)),
            pl.BlockSpec((1, D), lambda i: (0, 0)),
            pl.BlockSpec((None, 1, D), lambda i: (jnp.where(i >= nl, 1, 0), 0, 0)),
            pl.BlockSpec((None, 1, D), lambda i: (jnp.where(i >= nl, 1, 0), 0, 0)),
        ],
        out_specs=pl.BlockSpec((tr, D), lambda i: (i, 0)),
        out_shape=jax.ShapeDtypeStruct((rows, D), BF16),
        compiler_params=_cparams(("parallel",), vm),
        name="norm_mod",
    )(x, g.reshape(1, D), shift2, scale2)


def _final_norm_kernel(x_ref, g_ref, o_ref):
    x = x_ref[...]
    o_ref[...] = x * lax.rsqrt(jnp.mean(x * x, axis=-1, keepdims=True) + NORM_EPS) * g_ref[...]


def _final_norm(x, g, rows):
    D = x.shape[1]
    tr = ROW_TILE
    return pl.pallas_call(
        _final_norm_kernel,
        grid=(rows // tr,),
        in_specs=[pl.BlockSpec((tr, D), lambda i: (i, 0)), pl.BlockSpec((1, D), lambda i: (0, 0))],
        out_specs=pl.BlockSpec((tr, D), lambda i: (i, 0)),
        out_shape=jax.ShapeDtypeStruct((rows, D), F32),
        compiler_params=_cparams(("parallel",), 4 * tr * D * 4),
        name="final_norm",
    )(x, g.reshape(1, D))


def _mm_kernel(*refs, nk, epi, tm, n_lat, transposed):
    if nk > 1:
        acc_ref = refs[-1]
        refs = refs[:-1]
    a_ref, w_ref = refs[0], refs[1]
    o_ref = refs[-1]
    extra = refs[2:-1]
    row0 = pl.program_id(0) * tm

    def finish(acc):
        if epi == "cast":
            o_ref[...] = acc.astype(o_ref.dtype)
        elif epi == "sigmoid_bias":
            o_ref[...] = _sigmoid(acc + extra[0][...]).astype(o_ref.dtype)
        elif epi == "residual":
            res_ref, gate_ref = extra
            row = row0 + lax.broadcasted_iota(jnp.int32, acc.shape, 0)
            gate = jnp.where(row < n_lat, gate_ref[0:1, :], gate_ref[1:2, :])
            o_ref[...] = res_ref[...] + gate * acc
        else:
            raise ValueError(epi)

    w = w_ref[...].astype(BF16)
    if transposed:
        part = lax.dot_general(a_ref[...], w, (((1,), (1,)), ((), ())), preferred_element_type=F32)
    else:
        part = jnp.dot(a_ref[...], w, preferred_element_type=F32)
    if nk == 1:
        finish(part)
        return
    k = pl.program_id(2)

    @pl.when(k == 0)
    def _():
        acc_ref[...] = part

    @pl.when(k > 0)
    def _():
        acc_ref[...] += part

    @pl.when(k == nk - 1)
    def _():
        finish(acc_ref[...])


def _mm_tiles(rows, n, col0, K, osz, wsz, residual, any_col0):
    best = None
    for tm in (1408, 1024, 768, 512, 256, 128, 64):
        if rows % tm:
            continue
        for tn in (2048, 1024, 512, 256, 128):
            if n % tn or (col0 % tn and not any_col0):
                continue
            for tk in ((K,) if K <= MAX_FULL_K else range(LANES, MAX_TK + 1, LANES)):
                if K % tk:
                    continue
                vm = 2 * (tm * tk * 2 + tk * tn * wsz + tm * tn * osz) + tm * tn * 4
                vm += tk * tn * 2 if wsz > 2 else 0
                vm += tm * tn * 4 if tk < K else 0
                vm += 2 * tm * tn * 4 if residual else 0
                if vm > MM_VMEM_BUDGET:
                    continue
                intensity = tm * tn / (tm + tn)
                key = (min(intensity, MM_INTENSITY_ENOUGH), tk, intensity)
                if best is None or key > best[0]:
                    best = (key, (tm, tn, tk), vm)
    assert best is not None
    return best[1], best[2]


def _matmul(a, w, layer, *, rows, out_dtype, col0=0, n=None, transposed=False, epi="cast", bias=None, res=None,
            gate2=None, n_lat=0, name="mm"):
    K = w.shape[2] if transposed else w.shape[1]
    n = (w.shape[1] if transposed else w.shape[2]) - col0 if n is None else n
    (tm, tn, tk), vm = _mm_tiles(rows, n, col0, K, jnp.dtype(out_dtype).itemsize, w.dtype.itemsize,
                                 epi == "residual", transposed)
    nk = K // tk
    if transposed:
        assert nk == 1 and col0 % SUBLANES_F32 == 0
        w_spec = pl.BlockSpec((None, pl.Element(tn), pl.Element(K)),
                              lambda i, j, k: (layer, pl.multiple_of(col0 + j * tn, SUBLANES_F32), 0))
    else:
        cb = col0 // tn
        w_spec = pl.BlockSpec((None, tk, tn), lambda i, j, k: (layer, k, cb + j))
    in_specs = [pl.BlockSpec((tm, tk), lambda i, j, k: (i, k)), w_spec]
    args = [a, w]
    if epi == "sigmoid_bias":
        in_specs.append(pl.BlockSpec((1, tn), lambda i, j, k: (0, j)))
        args.append(bias.reshape(1, n))
    elif epi == "residual":
        in_specs.append(pl.BlockSpec((tm, tn), lambda i, j, k: (i, j)))
        in_specs.append(pl.BlockSpec((2, tn), lambda i, j, k: (0, j)))
        args += [res, gate2]
    scratch = [pltpu.VMEM((tm, tn), F32)] if nk > 1 else []
    return pl.pallas_call(
        functools.partial(_mm_kernel, nk=nk, epi=epi, tm=tm, n_lat=n_lat, transposed=transposed),
        grid=(rows // tm, n // tn, nk),
        in_specs=in_specs,
        out_specs=pl.BlockSpec((tm, tn), lambda i, j, k: (i, j)),
        out_shape=jax.ShapeDtypeStruct((rows, n), out_dtype),
        scratch_shapes=scratch,
        compiler_params=_cparams(("parallel", "parallel", "arbitrary"), vm),
        name=name,
    )(*args)


def _qk_prep_kernel(zq_ref, zk_ref, c_ref, sa_ref, sb_ref, gq_ref, gk_ref, q_ref, k_ref, *, hd):
    cos, sa, sb = c_ref[...], sa_ref[...], sb_ref[...]

    def norm_rope(x, g):
        y = x * lax.rsqrt(jnp.mean(x * x, axis=-1, keepdims=True) + NORM_EPS) * g
        return y * cos + pltpu.roll(y, hd - hd // 4, 1) * sa + pltpu.roll(y, hd // 4, 1) * sb

    gq = gq_ref[...] * (hd ** -0.5 * LOG2_E)
    for h in range(zq_ref.shape[1] // hd):
        sl = slice(h * hd, (h + 1) * hd)
        q_ref[:, sl] = norm_rope(zq_ref[:, sl].astype(F32), gq).astype(BF16)
    gk = gk_ref[...]
    for h in range(zk_ref.shape[1] // hd):
        sl = slice(h * hd, (h + 1) * hd)
        k_ref[:, sl] = norm_rope(zk_ref[:, sl].astype(F32), gk).astype(BF16)


def _qk_prep(z1, tabs, gq, gk, aw, kvw, hd):
    R = z1.shape[0]
    tr = ROW_TILE
    assert aw % kvw == 0
    cos, sa, sb = tabs
    tab_spec = pl.BlockSpec((tr, hd), lambda i: (i, 0))
    vm = 2 * 2 * (tr * aw * 2 + tr * kvw * 2) + 6 * tr * hd * 4 + 8 * tr * hd * 4
    return pl.pallas_call(
        functools.partial(_qk_prep_kernel, hd=hd),
        grid=(R // tr,),
        in_specs=[
            pl.BlockSpec((tr, aw), lambda i: (i, 0)),
            pl.BlockSpec((tr, kvw), lambda i: (i, aw // kvw)),
            tab_spec, tab_spec, tab_spec,
            pl.BlockSpec((1, hd), lambda i: (0, 0)),
            pl.BlockSpec((1, hd), lambda i: (0, 0)),
        ],
        out_specs=[pl.BlockSpec((tr, aw), lambda i: (i, 0)), pl.BlockSpec((tr, kvw), lambda i: (i, 0))],
        out_shape=[jax.ShapeDtypeStruct((R, aw), BF16), jax.ShapeDtypeStruct((R, kvw), BF16)],
        compiler_params=_cparams(("parallel",), vm),
        name="qk_prep",
    )(z1, z1, cos, sa, sb, gq.reshape(1, hd), gk.reshape(1, hd))


def _attn_kernel(q_ref, k_ref, v_ref, o_ref, *, hd, tk, nkv):
    tq = q_ref.shape[0]
    q4 = jnp.concatenate([q_ref[:, g * hd:(g + 1) * hd] for g in range(Q_PER_KV)], axis=0)

    ones_col = jnp.where(lax.broadcasted_iota(jnp.int32, (tk, hd), 1) == 0, 1.0, 0.0).astype(v_ref.dtype)
    m = acc = None
    for j in range(nkv):
        kc = k_ref[j * tk:(j + 1) * tk, :]
        ve = jnp.concatenate([v_ref[j * tk:(j + 1) * tk, :], ones_col], axis=1)
        s = lax.dot_general(q4, kc, (((1,), (1,)), ((), ())), preferred_element_type=F32)
        smax = jnp.max(s, axis=-1, keepdims=True)
        if j == 0:
            m = smax
            acc = jnp.dot(jnp.exp2(s - m).astype(BF16), ve, preferred_element_type=F32)
        else:
            m_new = jnp.maximum(m, smax)
            alpha = jnp.exp2(m - m_new)
            acc = alpha * acc + jnp.dot(jnp.exp2(s - m_new).astype(BF16), ve, preferred_element_type=F32)
            m = m_new
    o = acc[:, :hd] / acc[:, hd:hd + 1]
    for g in range(Q_PER_KV):
        o_ref[:, g * hd:(g + 1) * hd] = o[g * tq:(g + 1) * tq].astype(o_ref.dtype)


def _attention(q, k, z1, *, q_row0, q_rows, kv_row0, kv_rows, v_col0, hd):
    aw, kvw = q.shape[1], k.shape[1]
    n_kv = kvw // hd
    gw = Q_PER_KV * hd
    tq = _pick(q_rows, (256, 128))
    tk = _pick(kv_rows, (768, 512, 256, 128))
    assert q_row0 % tq == 0 and kv_row0 % kv_rows == 0 and v_col0 % hd == 0
    qb0, kb0, vc0 = q_row0 // tq, kv_row0 // kv_rows, v_col0 // hd
    vm = 2 * (2 * tq * gw * 2 + 2 * kv_rows * hd * 2) + 6 * Q_PER_KV * tq * tk * 4
    return pl.pallas_call(
        functools.partial(_attn_kernel, hd=hd, tk=tk, nkv=kv_rows // tk),
        grid=(n_kv, q_rows // tq),
        in_specs=[
            pl.BlockSpec((tq, gw), lambda h, i: (qb0 + i, h)),
            pl.BlockSpec((kv_rows, hd), lambda h, i: (kb0, h)),
            pl.BlockSpec((kv_rows, hd), lambda h, i: (kb0, vc0 + h)),
        ],
        out_specs=pl.BlockSpec((tq, gw), lambda h, i: (i, h)),
        out_shape=jax.ShapeDtypeStruct((q_rows, aw), BF16),
        compiler_params=_cparams(("parallel", "parallel"), vm),
        name="attention",
    )(q, k, z1)


def _halo_specs(tr, hr, width, col_block, n_rows):
    per = tr // hr
    last = n_rows // hr - 1
    prev = pl.BlockSpec((hr, width), lambda i, *_: (jnp.maximum(i * per - 1, 0), col_block(*_)))
    nxt = pl.BlockSpec((hr, width), lambda i, *_: (jnp.minimum((i + 1) * per, last), col_block(*_)))
    return prev, nxt


def _edge_flags(i, tr, n_lat, n_rows):
    r0 = i * tr
    r1 = r0 + tr
    pm = jnp.where((r0 == 0) | (r0 == n_lat), 0.0, 1.0).astype(F32)
    nm = jnp.where((r1 == n_lat) | (r1 == n_rows), 0.0, 1.0).astype(F32)
    return pm, nm


def _dwconv3(x, prev_row, next_row, w):
    tr = x.shape[0]
    rid = lax.broadcasted_iota(jnp.int32, x.shape, 0)
    xp = jnp.where(rid == 0, prev_row, pltpu.roll(x, 1, 0))
    xn = jnp.where(rid == tr - 1, next_row, pltpu.roll(x, tr - 1, 0))
    return xp * w[0:1] + x * w[1:2] + xn * w[2:3]


def _conv_branch_kernel(h_ref, hp_ref, hn_ref, b_ref, c_ref, cp_ref, cn_ref, w_ref, o_ref, *, n_lat, n_rows):
    tr = h_ref.shape[0]
    pm, nm = _edge_flags(pl.program_id(0), tr, n_lat, n_rows)
    hr = hp_ref.shape[0]
    u = c_ref[...].astype(F32) * h_ref[...].astype(F32)
    up = cp_ref[hr - 1:hr, :].astype(F32) * hp_ref[hr - 1:hr, :].astype(F32) * pm
    un = cn_ref[0:1, :].astype(F32) * hn_ref[0:1, :].astype(F32) * nm
    o_ref[...] = (b_ref[...].astype(F32) * _dwconv3(u, up, un, w_ref[...])).astype(o_ref.dtype)


def _conv_branch(z1, conv_w, *, rows, col0, cw, n_lat):
    tr, hr = ROW_TILE, SUBLANES_BF16
    assert col0 % cw == 0
    cb = col0 // cw
    main = lambda o: pl.BlockSpec((tr, cw), lambda i: (i, cb + o))
    hp, hn = _halo_specs(tr, hr, cw, lambda: cb, rows)
    cp, cn = _halo_specs(tr, hr, cw, lambda: cb + 2, rows)
    vm = 2 * 4 * tr * cw * 2 + 8 * tr * cw * 4
    return pl.pallas_call(
        functools.partial(_conv_branch_kernel, n_lat=n_lat, n_rows=rows),
        grid=(rows // tr,),
        in_specs=[main(0), hp, hn, main(1), main(2), cp, cn, pl.BlockSpec((3, cw), lambda i: (0, 0))],
        out_specs=pl.BlockSpec((tr, cw), lambda i: (i, 0)),
        out_shape=jax.ShapeDtypeStruct((rows, cw), BF16),
        compiler_params=_cparams(("parallel",), vm),
        name="conv_branch",
    )(z1, z1, z1, z1, z1, z1, z1, conv_w)


def _ffn_act_kernel(v_ref, vp_ref, vn_ref, g_ref, gp_ref, gn_ref, wv_ref, wg_ref, bv_ref, bg_ref, o_ref, *,
                    n_lat, n_rows):
    tr = v_ref.shape[0]
    pm, nm = _edge_flags(pl.program_id(0), tr, n_lat, n_rows)
    hr = vp_ref.shape[0]
    ri = lax.broadcasted_iota(jnp.int32, (2 * tr, tr), 0)
    ci = lax.broadcasted_iota(jnp.int32, (2 * tr, tr), 1)
    shift = jnp.where(ci == jnp.where(ri < tr, ri - 1, ri - tr + 1), 1.0, 0.0).astype(v_ref.dtype)
    r8 = lax.broadcasted_iota(jnp.int32, (SUBLANES_F32, v_ref.shape[1]), 0)

    def conv(x_ref, p_ref, n_ref, w_ref, b_ref):
        x = x_ref[...]
        sh = jnp.dot(shift, x, preferred_element_type=F32)
        first = sh[:SUBLANES_F32] + jnp.where(r8 == 0, p_ref[hr - 1:hr, :].astype(F32) * pm, 0.0)
        lastr = sh[2 * tr - SUBLANES_F32:] + jnp.where(r8 == SUBLANES_F32 - 1, n_ref[0:1, :].astype(F32) * nm, 0.0)
        xp = jnp.concatenate([first, sh[SUBLANES_F32:tr]], axis=0)
        xn = jnp.concatenate([sh[tr:2 * tr - SUBLANES_F32], lastr], axis=0)
        w = w_ref[...]
        return xp * w[0:1] + x.astype(F32) * w[1:2] + xn * w[2:3] + b_ref[...]

    val = conv(v_ref, vp_ref, vn_ref, wv_ref, bv_ref)
    gate = conv(g_ref, gp_ref, gn_ref, wg_ref, bg_ref)
    tc = v_ref.shape[1]
    o_ref[:, :tc] = (_silu(gate) * val).astype(o_ref.dtype)
    if o_ref.shape[1] > tc:
        o_ref[:, tc:] = jnp.zeros((tr, o_ref.shape[1] - tc), o_ref.dtype)


def _ffn_blocks(ff):
    tc = max(t for t in range(LANES, MAX_FFN_COLS + 1, LANES) if ff % t == 0)
    nb = ff // tc
    pad = next(p for p in range(0, FFN_K_UNIT + 1, LANES) if (nb * (tc + p)) % FFN_K_UNIT == 0)
    return tc, nb, pad


def _ffn_act(u, conv_w, conv_b, *, rows, n_lat):
    ff = u.shape[1] // 2
    tr, hr = ROW_TILE // 2, SUBLANES_BF16
    tc, nb, pad = _ffn_blocks(ff)
    vp, vn = _halo_specs(tr, hr, tc, lambda j: j, rows)
    gp, gn = _halo_specs(tr, hr, tc, lambda j: nb + j, rows)
    wspec = lambda o: pl.BlockSpec((3, tc), lambda i, j: (0, o * nb + j))
    bspec = lambda o: pl.BlockSpec((1, tc), lambda i, j: (0, o * nb + j))
    vm = 2 * 3 * tr * tc * 2 + 12 * tr * tc * 4
    return pl.pallas_call(
        functools.partial(_ffn_act_kernel, n_lat=n_lat, n_rows=rows),
        grid=(rows // tr, nb),
        in_specs=[
            pl.BlockSpec((tr, tc), lambda i, j: (i, j)), vp, vn,
            pl.BlockSpec((tr, tc), lambda i, j: (i, nb + j)), gp, gn,
            wspec(0), wspec(1), bspec(0), bspec(1),
        ],
        out_specs=pl.BlockSpec((tr, tc + pad), lambda i, j: (i, j)),
        out_shape=jax.ShapeDtypeStruct((rows, nb * (tc + pad)), BF16),
        compiler_params=_cparams(("parallel", "parallel"), vm),
        name="ffn_act",
    )(u, u, u, u, u, u, conv_w, conv_w, conv_b, conv_b)


def _head_sums(x, hs):
    assert hs & (hs - 1) == 0
    li = lax.broadcasted_iota(jnp.int32, (LANES, LANES), 0) & -hs
    lj = lax.broadcasted_iota(jnp.int32, (LANES, LANES), 1) & -hs
    seg = jnp.where(li == lj, 1.0, 0.0).astype(F32)
    parts = [jnp.dot(x[:, t * LANES:(t + 1) * LANES], seg, preferred_element_type=F32,
                     precision=lax.Precision.HIGHEST) for t in range(x.shape[1] // LANES)]
    return jnp.concatenate(parts, axis=1)


def _rwkv_feat_kernel(z_ref, zp_ref, zn_ref, lo_ref, cw_ref, w2_ref, w0_ref, a2_ref, a0_ref, g2_ref, kk_ref,
                      ka_ref, r_o, v_o, kn_o, ld0_o, ld1_o, kd0_o, kd1_o, b0_o, b1_o, g_o, *,
                      n_lat, n_rows, dl2, il2, hs):
    tr, w3 = z_ref.shape
    W = w3 // 3
    pm, nm = _edge_flags(pl.program_id(0), tr, n_lat, n_rows)
    hr = zp_ref.shape[0]
    rkv = _dwconv3(z_ref[...].astype(F32), zp_ref[hr - 1:hr, :].astype(F32) * pm,
                   zn_ref[0:1, :].astype(F32) * nm, cw_ref[...])
    r, k, v = rkv[:, :W], rkv[:, W:2 * W], rkv[:, 2 * W:]
    lw = lo_ref[:, :dl2]
    la = lo_ref[:, dl2:dl2 + il2]
    lg = lo_ref[:, dl2 + il2:]
    wl = jnp.dot(jnp.tanh(lw).astype(BF16), w2_ref[...], preferred_element_type=F32) + w0_ref[...]
    ld = -jnp.exp(-_softplus(-wl) - 0.5)
    a = _sigmoid(jnp.dot(la.astype(BF16), a2_ref[...], preferred_element_type=F32) + a0_ref[...])
    g = jnp.dot(_sigmoid(lg).astype(BF16), g2_ref[...], preferred_element_type=F32)
    kk = k * kk_ref[...]
    kn = kk / jnp.maximum(jnp.sqrt(_head_sums(kk * kk, hs)), 1e-12)
    ka = ka_ref[...]
    r_o[...] = r
    v_o[...] = v
    kn_o[...] = kn
    g_o[...] = g
    for d, (ld_o, kd_o, b_o) in enumerate(((ld0_o, kd0_o, b0_o), (ld1_o, kd1_o, b1_o))):
        ad = a[:, d * W:(d + 1) * W]
        ld_o[...] = ld[:, d * W:(d + 1) * W]
        kd_o[...] = k * (1.0 + (ad - 1.0) * ka)
        b_o[...] = kn * ad


def _rwkv_features(z2, lo, p, *, n_lat, hs):
    R, w3 = z2.shape
    W = w3 // 3
    lwp = lo.shape[1]
    tr, hr = ROW_TILE // 2, SUBLANES_BF16
    dl2, il2 = p["w2"].shape[0], p["a2"].shape[0]
    zp, zn = _halo_specs(tr, hr, w3, lambda: 0, R)
    full = lambda a: pl.BlockSpec(a.shape, lambda i: (0,) * a.ndim)
    out_spec = pl.BlockSpec((tr, W), lambda i: (i, 0))
    vm = 2 * (tr * w3 * 4 + tr * lwp * 4 + 10 * tr * W * 4) + 12 * tr * w3 * 4
    outs = pl.pallas_call(
        functools.partial(_rwkv_feat_kernel, n_lat=n_lat, n_rows=R, dl2=dl2, il2=il2, hs=hs),
        grid=(R // tr,),
        in_specs=[pl.BlockSpec((tr, w3), lambda i: (i, 0)), zp, zn, pl.BlockSpec((tr, lwp), lambda i: (i, 0)),
                  full(p["cw"]), full(p["w2"]), full(p["w0"]), full(p["a2"]), full(p["a0"]), full(p["g2"]),
                  full(p["k_k"]), full(p["k_a"])],
        out_specs=[out_spec] * 10,
        out_shape=[jax.ShapeDtypeStruct((R, W), F32)] * 10,
        compiler_params=_cparams(("parallel",), vm),
        name="rwkv_features",
    )(z2, z2, z2, lo, p["cw"], p["w2"], p["w0"], p["a2"], p["a0"], p["g2"], p["k_k"], p["k_a"])
    return dict(zip(("r", "v", "kn", "ld0", "ld1", "kd0", "kd1", "b0", "b1", "g"), outs))


def _scan_kernel(rf, vf, kf, ldf, kdf, bf, rb, vb, kb, ldb, kdb, bb, yf_o, yb_o, st_ref):
    @pl.when(pl.program_id(0) == 0)
    def _():
        st_ref[...] = jnp.zeros_like(st_ref)

    C, W = rf.shape
    C2 = 2 * C
    dot = functools.partial(jnp.dot, preferred_element_type=F32)
    dot_nt = lambda a, b: lax.dot_general(a, b, (((1,), (1,)), ((), ())), preferred_element_type=F32)
    ri = lax.broadcasted_iota(jnp.int32, (C, C), 0)
    ci = lax.broadcasted_iota(jnp.int32, (C, C), 1)
    ii = lax.broadcasted_iota(jnp.int32, (C2, C2), 0)
    jj = lax.broadcasted_iota(jnp.int32, (C2, C2), 1)
    im, jm = ii & (C - 1), jj & (C - 1)
    eye = jnp.where(ii == jj, 1.0, 0.0).astype(F32)
    same_head = (ii & -C) == (jj & -C)
    h0 = lax.broadcasted_iota(jnp.int32, (C, LANES), 1) < LANES // 2

    def stack(x):
        return jnp.concatenate([jnp.where(h0, x, 0.0), jnp.where(h0, 0.0, x)], axis=0)

    def fold(x):
        return x[:C] + x[C:]

    inst = []
    for d, (r_, v_, k_, ld_, kd_, b_, y_o) in enumerate(((rf, vf, kf, ldf, kdf, bf, yf_o),
                                                          (rb, vb, kb, ldb, kdb, bb, yb_o))):
        rev = d == 1
        ld, kd, b = ld_[...], kd_[...], b_[...]
        inc = jnp.where((ci >= ri) if rev else (ci <= ri), 1.0, 0.0).astype(F32)
        cum = jnp.dot(inc, ld, preferred_element_type=F32, precision=lax.Precision.HIGHEST)
        tot = cum[0:1, :] if rev else cum[C - 1:C, :]
        e_neg = jnp.exp(-cum)
        e_dec = jnp.exp(tot - cum)
        full = dict(kt=k_[...] * jnp.exp(cum - ld), rt=r_[...] * jnp.exp(cum), kh=kd * e_neg, bh=b * e_neg,
                    kdec=kd * e_dec, bdec=b * e_dec, v=v_[...], etot=jnp.exp(tot))
        strict = (jm > im) if rev else (jm < im)
        incl = (jm >= im) if rev else (jm <= im)
        for p in range(W // LANES):
            q = {n: a[:, p * LANES:(p + 1) * LANES] for n, a in full.items()}
            q.update(d=d, p=p, rev=rev, strict=strict, incl=incl, y_o=y_o)
            inst.append(q)

    for q in inst:
        lhs = jnp.concatenate([stack(q["kt"]), stack(q["rt"])], axis=0).astype(BF16)
        rhs = jnp.concatenate([stack(q["bh"]), stack(q["kh"])], axis=0).astype(BF16)
        q["G"] = dot_nt(lhs, rhs)
    for q in inst:
        G = q.pop("G")
        q["Lb"] = jnp.where(q["strict"], G[:C2, :C2], 0.0)
        q["LkAk"] = jnp.concatenate([jnp.where(q["strict"], G[:C2, C2:], 0.0),
                                     jnp.where(q["incl"], G[C2:, C2:], 0.0)], axis=0).astype(BF16)
        q["Ab"] = jnp.where(q["incl"], G[C2:, :C2], 0.0).astype(BF16)

    s = 1
    while s < C:
        same = (ii & -(2 * s)) == (jj & -(2 * s))
        hi_i, hi_j = (ii & s) != 0, (jj & s) != 0
        off = {False: same & hi_i & ~hi_j, True: same & ~hi_i & hi_j}
        if s == 1:
            for q in inst:
                q["T"] = eye - jnp.where(off[q["rev"]], q["Lb"], 0.0)
        else:
            for q in inst:
                q["Tb"] = q["T"].astype(BF16)
                q["TM"] = dot(q["Tb"], jnp.where(off[q["rev"]], q["Lb"], 0.0).astype(BF16))
            for q in inst:
                q["T"] = q["T"] - dot(q.pop("TM").astype(BF16), q.pop("Tb"))
        s *= 2

    for q in inst:
        S = st_ref[q["d"], q["p"]]
        q["S"] = S
        q["X"] = dot_nt(jnp.concatenate([q["kt"], q["rt"]], axis=0).astype(BF16), S.astype(BF16))
        q["LA"] = dot(q.pop("LkAk"), stack(q["v"]).astype(BF16))
    for q in inst:
        q["Us"] = dot(q.pop("T").astype(BF16), (stack(q["X"][:C]) + q["LA"][:C2]).astype(BF16))
    for q in inst:
        q["ABU"] = dot(q.pop("Ab"), q["Us"].astype(BF16))
    for q in inst:
        sl = slice(q["p"] * LANES, (q["p"] + 1) * LANES)
        q["y_o"][:, sl] = q["X"][C:] + fold(q["LA"][C2:] - q.pop("ABU"))
        vu_t = jnp.concatenate([q["v"], -fold(q["Us"])], axis=0).T.astype(BF16)
        upd = dot(vu_t, jnp.concatenate([q["kdec"], q["bdec"]], axis=0).astype(BF16))
        st_ref[q["d"], q["p"]] = q["S"] * q["etot"] + jnp.where(same_head, upd, 0.0)


def _rwkv_scan(f, *, n_lat):
    R, W = f["r"].shape
    C = CHUNK
    n, nl = R // C, n_lat // C
    nc = n - nl
    fwd = lambda s: (jnp.where(s < nc, nl + s, s - nc), 0)
    bwd = lambda s: (n - 1 - s, 0)
    fs, bs = pl.BlockSpec((C, W), fwd), pl.BlockSpec((C, W), bwd)
    inst_tmp = 24 * LANES * LANES * 4
    vm = 2 * 14 * C * W * 4 + 2 * (W // LANES) * (LANES * LANES * 4 + inst_tmp)
    return pl.pallas_call(
        _scan_kernel,
        grid=(n,),
        in_specs=[fs] * 6 + [bs] * 6,
        out_specs=[fs, bs],
        out_shape=[jax.ShapeDtypeStruct((R, W), F32)] * 2,
        scratch_shapes=[pltpu.VMEM((2, W // LANES, LANES, LANES), F32)],
        compiler_params=_cparams(("arbitrary",), vm),
        name="rwkv_scan",
    )(f["r"], f["v"], f["kn"], f["ld0"], f["kd0"], f["b0"], f["r"], f["v"], f["kn"], f["ld1"], f["kd1"], f["b1"])


def _rwkv_readout_kernel(yf_ref, yb_ref, r_ref, v_ref, kd0_ref, kd1_ref, g_ref, rk_ref, lw_ref, lb_ref, o_ref, *,
                         hs):
    y = yf_ref[...] + yb_ref[...]
    mu = _head_sums(y, hs) * (1.0 / hs)
    d = y - mu
    var = _head_sums(d * d, hs) * (1.0 / hs)
    yn = d * lax.rsqrt(var + GN_EPS) * lw_ref[...] + lb_ref[...]
    bonus = _head_sums(r_ref[...] * (kd0_ref[...] + kd1_ref[...]) * rk_ref[...], hs) * v_ref[...]
    o_ref[...] = ((yn + bonus) * g_ref[...]).astype(o_ref.dtype)


def _rwkv_readout(yf, yb, f, p, *, rows, hs):
    W = yf.shape[1]
    tr = ROW_TILE
    rs = pl.BlockSpec((tr, W), lambda i: (i, 0))
    ps = pl.BlockSpec((1, W), lambda i: (0, 0))
    vm = 2 * 8 * tr * W * 4 + 10 * tr * W * 4
    return pl.pallas_call(
        functools.partial(_rwkv_readout_kernel, hs=hs),
        grid=(rows // tr,),
        in_specs=[rs] * 7 + [ps] * 3,
        out_specs=rs,
        out_shape=jax.ShapeDtypeStruct((rows, W), BF16),
        compiler_params=_cparams(("parallel",), vm),
        name="rwkv_readout",
    )(yf, yb, f["r"], f["v"], f["kd0"], f["kd1"], f["g"], p["r_k"], p["ln_w"], p["ln_b"])


def _merge_kernel(a_ref, c_ref, r_ref, wa_ref, wc_ref, wr_ref, ga_ref, gc_ref, gr_ref, o_ref):
    ya = jnp.dot(a_ref[...], wa_ref[...], preferred_element_type=F32)
    yc = jnp.dot(c_ref[...], wc_ref[...], preferred_element_type=F32)
    yr = jnp.dot(r_ref[...], wr_ref[...], preferred_element_type=F32)
    o_ref[...] = (ga_ref[...].astype(F32) * ya + gc_ref[...].astype(F32) * yc
                  + gr_ref[...].astype(F32) * yr).astype(o_ref.dtype)


def _merge(att, cv, rw, wa, wc, wr, layer, gates, *, rows):
    D = wa.shape[2]
    tm = _pick(rows, (512, 256, 128, 64))
    tn = _pick(D, (1024, 512, 256, 128))
    nb = D // tn
    ka, kc, kr = wa.shape[1], wc.shape[1], wr.shape[1]
    vm = 2 * ((tm + tn) * (ka + kc + kr) * 2 + 4 * tm * tn * 2) + 4 * tm * tn * 4
    return pl.pallas_call(
        _merge_kernel,
        grid=(nb, rows // tm),
        in_specs=[
            pl.BlockSpec((tm, ka), lambda j, i: (i, 0)),
            pl.BlockSpec((tm, kc), lambda j, i: (i, 0)),
            pl.BlockSpec((tm, kr), lambda j, i: (i, 0)),
            pl.BlockSpec((None, ka, tn), lambda j, i: (layer, 0, j)),
            pl.BlockSpec((None, kc, tn), lambda j, i: (layer, 0, j)),
            pl.BlockSpec((None, kr, tn), lambda j, i: (layer, 0, j)),
            pl.BlockSpec((tm, tn), lambda j, i: (i, j)),
            pl.BlockSpec((tm, tn), lambda j, i: (i, nb + j)),
            pl.BlockSpec((tm, tn), lambda j, i: (i, 2 * nb + j)),
        ],
        out_specs=pl.BlockSpec((tm, tn), lambda j, i: (i, j)),
        out_shape=jax.ShapeDtypeStruct((rows, D), BF16),
        compiler_params=_cparams(("parallel", "parallel"), vm),
        name="merge",
    )(att, cv, rw, wa, wc, wr, gates, gates, gates)


def _rope_tables(n_lat, n_ctx, hd):
    rows = n_lat // GRID_W
    row = jnp.repeat(jnp.arange(rows, dtype=F32), GRID_W)
    col = jnp.tile(jnp.arange(GRID_W, dtype=F32), rows)
    half = hd // 2
    inv_freq = ROPE_THETA ** (-jnp.arange(0, half, 2, dtype=F32) / half)
    ar, ac = row[:, None] * inv_freq, col[:, None] * inv_freq
    z = jnp.zeros_like(ar)
    cos = jnp.concatenate([jnp.cos(ar), jnp.cos(ar), jnp.cos(ac), jnp.cos(ac)], axis=-1)
    sa = jnp.concatenate([-jnp.sin(ar), z, -jnp.sin(ac), z], axis=-1)
    sb = jnp.concatenate([z, jnp.sin(ar), z, jnp.sin(ac)], axis=-1)
    pad = lambda t, v: jnp.concatenate([t, jnp.full((n_ctx, hd), v, F32)], axis=0)
    return pad(cos, 1.0), pad(sa, 0.0), pad(sb, 0.0)


def _block_diag2(w):
    z = jnp.zeros_like(w[0])
    return jnp.concatenate([jnp.concatenate([w[0], z], axis=1), jnp.concatenate([z, w[1]], axis=1)], axis=0)


def kernel(x, c, ctx, c_ctx, w_mod, b_mod, norm1_g, norm2_g, w_in, b_gate, q_norm_g, k_norm_g, w_attn_o, conv_w, w_conv_o, rwkv_conv_w, rwkv_w0, rwkv_w2, rwkv_a0, rwkv_a2, rwkv_g2, rwkv_k_k, rwkv_k_a, rwkv_r_k, rwkv_ln_w, rwkv_ln_b, w_rwkv_o, w_out, ffn_w_up, ffn_conv_w, ffn_conv_b, ffn_w_down, final_g):
    B, S, D = x.shape
    assert B == 1, "batch is folded away; the problem fixes BATCH = 1"
    Tc = ctx.shape[1]
    R = S + Tc
    L = w_mod.shape[0]
    hd = q_norm_g.shape[1]
    aw = w_attn_o.shape[1]
    cw = conv_w.shape[2]
    W = rwkv_w0.shape[2]
    hs = rwkv_r_k.shape[2]
    dl, il, gl = rwkv_w2.shape[2], rwkv_a2.shape[2], rwkv_g2.shape[1]
    ff = ffn_w_down.shape[1]
    kvw = (w_in.shape[2] - (aw + 3 * cw + 3 * W + 2 * dl + 2 * il + gl + 3 * D)) // 2
    assert S % ROW_TILE == 0 and Tc % ROW_TILE == 0 and S % GRID_W == 0
    assert hd == LANES and 2 * hs == LANES and 2 * CHUNK == LANES and W % LANES == 0

    o_conv = aw + 2 * kvw
    o_rkv = o_conv + 3 * cw
    o_lora = o_rkv + 3 * W
    o_gate = o_lora + 2 * dl + 2 * il + gl
    lw_real = 2 * dl + 2 * il + gl
    lwp = -(-lw_real // LANES) * LANES
    gp = lwp - 2 * dl - 2 * il

    xa = jnp.concatenate([x[0], ctx[0]], axis=0)
    cond = jnp.zeros((16, D), F32).at[0].set(c[0]).at[1].set(c_ctx)
    mods = _ada_mod(cond, w_mod, b_mod)
    tabs = _rope_tables(S, Tc, hd)

    w_t = jnp.swapaxes(w_in, 1, 2)
    w_ao, w_co, w_ro, w_o = (w.astype(BF16) for w in (w_attn_o, w_conv_o, w_rwkv_o, w_out))
    f_tc, f_nb, f_pad = _ffn_blocks(ff)
    w_dn = jnp.pad(ffn_w_down.reshape(L, f_nb, f_tc, D), ((0, 0), (0, 0), (0, f_pad), (0, 0)))
    w_dn = w_dn.reshape(L, f_nb * (f_tc + f_pad), D).astype(BF16)

    for l in range(L):
        last = l == L - 1
        rows = S if last else R
        m = mods[l, :2].reshape(2, N_MOD, 1, D)
        mod = lambda i: m[:, i]
        rp = {
            "cw": rwkv_conv_w[l],
            "w2": _block_diag2(rwkv_w2[l]).astype(BF16),
            "w0": rwkv_w0[l].reshape(1, 2 * W),
            "a2": _block_diag2(rwkv_a2[l]).astype(BF16),
            "a0": rwkv_a0[l].reshape(1, 2 * W),
            "g2": jnp.pad(rwkv_g2[l], ((0, gp - gl), (0, 0))).astype(BF16),
            "k_k": rwkv_k_k[l].reshape(1, W),
            "k_a": rwkv_k_a[l].reshape(1, W),
            "r_k": rwkv_r_k[l].reshape(1, W),
            "ln_w": rwkv_ln_w[l].reshape(1, W),
            "ln_b": rwkv_ln_b[l].reshape(1, W),
        }

        h = _norm_mod(xa, norm1_g[l], mod(0), mod(1), S, R)
        in_proj = functools.partial(_matmul, h, w_t, l, transposed=True)
        z1 = in_proj(rows=R, out_dtype=BF16, col0=0, n=o_rkv, name="in_proj_qkv_conv")
        z2 = in_proj(rows=R, out_dtype=BF16, col0=o_rkv, n=3 * W, name="in_proj_rkv")
        lo = in_proj(rows=R, out_dtype=F32, col0=o_lora, n=lwp, name="in_proj_lora")
        gates = in_proj(rows=rows, out_dtype=BF16, col0=o_gate, n=3 * D, epi="sigmoid_bias", bias=b_gate[l],
                        name="in_proj_gates")

        qn, kn = _qk_prep(z1, tabs, q_norm_g[l], k_norm_g[l], aw, kvw, hd)
        att = _attention(qn, kn, z1, q_row0=0, q_rows=S, kv_row0=0, kv_rows=R, v_col0=aw + kvw, hd=hd)
        if not last:
            att_c = _attention(qn, kn, z1, q_row0=S, q_rows=Tc, kv_row0=S, kv_rows=Tc, v_col0=aw + kvw, hd=hd)
            att = jnp.concatenate([att, att_c], axis=0)

        cv = _conv_branch(z1, conv_w[l], rows=rows, col0=o_conv, cw=cw, n_lat=S)

        feat = _rwkv_features(z2, lo, rp, n_lat=S, hs=hs)
        yf, yb = _rwkv_scan(feat, n_lat=S)
        rw = _rwkv_readout(yf, yb, feat, rp, rows=rows, hs=hs)

        mg = _merge(att, cv, rw, w_ao, w_co, w_ro, l, gates, rows=rows)
        xa = _matmul(mg, w_o, l, rows=rows, out_dtype=F32, epi="residual", res=xa, gate2=mod(2).reshape(2, D),
                     n_lat=S, name="out_proj")

        h2 = _norm_mod(xa, norm2_g[l], mod(3), mod(4), S, rows)
        u = _matmul(h2, ffn_w_up, l, rows=rows, out_dtype=BF16, name="ffn_up")
        act = _ffn_act(u, ffn_conv_w[l], ffn_conv_b[l].reshape(1, 2 * ff), rows=rows, n_lat=S)
        xa = _matmul(act, w_dn, l, rows=rows, out_dtype=F32, epi="residual", res=xa, gate2=mod(5).reshape(2, D),
                     n_lat=S, name="ffn_down")

    return _final_norm(xa, final_g, S)[None]
```
